```python
import math
import jax, jax.numpy as jnp
from jax import lax
import numpy as np

D_MODEL = 1024
BATCH = 4
SEQ = 4096
DEPTH = 1

D_ATTN = 512
HEAD_DIM = 64
N_ATTN_HEADS = D_ATTN // HEAD_DIM
D_CONV = D_MODEL - D_ATTN
N_CONV_GROUPS = D_CONV // 64
CONV_WIDTH = 3
Q_BLOCK = 128
D_IN_PROJ = 3 * D_ATTN + 3 * D_CONV

N_GROUPS = 4
EXPERTS_PER_GROUP = 8
N_EXPERTS = N_GROUPS * EXPERTS_PER_GROUP
TOP_K_IN_GROUP = 2
D_EXPERT = 256

DEEPNORM_ALPHA = (2.0 * DEPTH) ** 0.25
DEEPNORM_BETA = (8.0 * DEPTH) ** -0.25
N_ADA = 6
LN_EPS = 1e-5

kernel_name = "hybrid_stickbreak_shortconv_hmoe_deepnorm_adaln"


def _layer_norm(x, g, b):
    xf = x.astype(jnp.float32)
    mu = jnp.mean(xf, axis=-1, keepdims=True)
    var = jnp.mean(jnp.square(xf - mu), axis=-1, keepdims=True)
    y = (xf - mu) * lax.rsqrt(var + LN_EPS)
    return (y * g.astype(jnp.float32) + b.astype(jnp.float32)).astype(x.dtype)


def _stick_breaking_attention(q, k, v):
    seq = q.shape[2]
    scale = HEAD_DIM ** -0.5
    outs = []
    for blk in range(seq // Q_BLOCK):
        start = blk * Q_BLOCK
        end = start + Q_BLOCK
        qb = q[:, :, start:end].astype(jnp.float32)
        kb = k[:, :, :end].astype(jnp.float32)
        vb = v[:, :, :end]
        z = jnp.einsum('bhqd,bhkd->bhqk', qb, kb) * scale
        t_pos = start + jnp.arange(Q_BLOCK)
        s_pos = jnp.arange(end)
        causal = s_pos[None, :] < t_pos[:, None]
        log_stay = jnp.where(causal, jax.nn.log_sigmoid(-z), 0.0)
        log_stay_after = lax.cumsum(log_stay, axis=3, reverse=True) - log_stay
        a = jnp.where(causal, jnp.exp(jax.nn.log_sigmoid(z) + log_stay_after), 0.0)
        outs.append(jnp.einsum('bhqk,bhkd->bhqd', a.astype(vb.dtype), vb))
    return jnp.concatenate(outs, axis=2)


def _causal_depthwise_conv(u, w):
    seq = u.shape[1]
    up = jnp.pad(u, ((0, 0), (CONV_WIDTH - 1, 0), (0, 0)))
    return sum(w[kk][None, None, :] * up[:, kk:kk + seq] for kk in range(CONV_WIDTH))


def _hierarchical_moe(h, w_rg, b_rg, w_re, b_re, w_gate, w_up, w_down):
    t = h.shape[0]
    hf = h.astype(jnp.float32)
    group_logits = hf @ w_rg.astype(jnp.float32) + b_rg.astype(jnp.float32)
    group_probs = jax.nn.softmax(group_logits, axis=-1)
    g_prob, g_idx = lax.top_k(group_probs, 1)
    exp_logits = (hf @ w_re.astype(jnp.float32) + b_re.astype(jnp.float32))
    exp_logits = exp_logits.reshape(t, N_GROUPS, EXPERTS_PER_GROUP)
    sel_logits = jnp.take_along_axis(exp_logits, g_idx[:, :, None], axis=1)[:, 0]
    sel_probs = jax.nn.softmax(sel_logits, axis=-1)
    top_p, top_i = lax.top_k(sel_probs, TOP_K_IN_GROUP)
    top_w = top_p / jnp.sum(top_p, axis=-1, keepdims=True) * g_prob
    in_group = jnp.sum(jax.nn.one_hot(top_i, EXPERTS_PER_GROUP, dtype=jnp.float32)
                       * top_w[:, :, None], axis=1)
    combine = jax.nn.one_hot(g_idx[:, 0], N_GROUPS, dtype=jnp.float32)[:, :, None] \
        * in_group[:, None, :]
    combine = combine.astype(h.dtype)
    w_gate = w_gate.reshape(N_GROUPS, EXPERTS_PER_GROUP, D_MODEL, D_EXPERT)
    w_up = w_up.reshape(N_GROUPS, EXPERTS_PER_GROUP, D_MODEL, D_EXPERT)
    w_down = w_down.reshape(N_GROUPS, EXPERTS_PER_GROUP, D_EXPERT, D_MODEL)
    y = jnp.zeros_like(h)
    for g in range(N_GROUPS):
        hid = jax.nn.silu(jnp.einsum('td,edf->tef', h, w_gate[g])) \
            * jnp.einsum('td,edf->tef', h, w_up[g])
        y = y + jnp.einsum('tef,te,efd->td', hid, combine[:, g], w_down[g])
    return y


def setup_inputs(seed: int = 0) -> dict:
    key = jax.random.key(seed)
    ks = jax.random.split(key, 20)
    f32 = jnp.float32
    d = D_MODEL
    x = jax.random.normal(ks[0], (BATCH, SEQ, d), f32)
    c = jax.random.normal(ks[1], (BATCH, d), f32)
    w_ada = jax.random.normal(ks[2], (DEPTH, d, N_ADA * d), f32) * (0.1 * d ** -0.5)
    ada_noise = 0.01 * jax.random.normal(ks[3], (DEPTH, N_ADA, d), f32)
    gate_bias = jnp.array([0.0, 0.0, 1.0, 0.0, 0.0, 1.0], f32)[None, :, None]
    b_ada = (ada_noise + gate_bias).reshape(DEPTH, N_ADA * d)
    w_in = jax.random.normal(ks[4], (DEPTH, d, D_IN_PROJ), f32) * d ** -0.5
    conv_w = jax.random.normal(ks[5], (DEPTH, CONV_WIDTH, D_CONV), f32) * CONV_WIDTH ** -0.5
    w_out = jax.random.normal(ks[6], (DEPTH, d, d), f32) * (d ** -0.5 * DEEPNORM_BETA)
    ln1_g = 1.0 + 0.01 * jax.random.normal(ks[7], (DEPTH, d), f32)
    ln1_b = 0.01 * jax.random.normal(ks[8], (DEPTH, d), f32)
    w_router_group = jax.random.normal(ks[9], (DEPTH, d, N_GROUPS), f32) * d ** -0.5
    b_router_group = 0.01 * jax.random.normal(ks[10], (DEPTH, N_GROUPS), f32)
    w_router_expert = jax.random.normal(ks[11], (DEPTH, d, N_EXPERTS), f32) * d ** -0.5
    b_router_expert = 0.01 * jax.random.normal(ks[12], (DEPTH, N_EXPERTS), f32)
    w_gate = jax.random.normal(ks[13], (DEPTH, N_EXPERTS, d, D_EXPERT), f32) * d ** -0.5
    w_up = jax.random.normal(ks[14], (DEPTH, N_EXPERTS, d, D_EXPERT), f32) * d ** -0.5
    w_down = jax.random.normal(ks[15], (DEPTH, N_EXPERTS, D_EXPERT, d), f32) \
        * (D_EXPERT ** -0.5 * DEEPNORM_BETA)
    ln2_g = 1.0 + 0.01 * jax.random.normal(ks[16], (DEPTH, d), f32)
    ln2_b = 0.01 * jax.random.normal(ks[17], (DEPTH, d), f32)
    return {"x": x, "c": c, "w_ada": w_ada, "b_ada": b_ada, "w_in": w_in,
            "conv_w": conv_w, "w_out": w_out, "ln1_g": ln1_g, "ln1_b": ln1_b,
            "w_router_group": w_router_group, "b_router_group": b_router_group,
            "w_router_expert": w_router_expert, "b_router_expert": b_router_expert,
            "w_gate": w_gate, "w_up": w_up, "w_down": w_down,
            "ln2_g": ln2_g, "ln2_b": ln2_b}


def reference(x, c, w_ada, b_ada, w_in, conv_w, w_out, ln1_g, ln1_b,
              w_router_group, b_router_group, w_router_expert, b_router_expert,
              w_gate, w_up, w_down, ln2_g, ln2_b):
    bsz, seq, d = x.shape
    for l in range(DEPTH):
        ada = (jax.nn.silu(c) @ w_ada[l] + b_ada[l]).reshape(bsz, N_ADA, d)
        shift1, scale1, gate1, shift2, scale2, gate2 = [ada[:, i][:, None, :] for i in range(N_ADA)]

        h = x * (1.0 + scale1) + shift1
        proj = h @ w_in[l]
        q, k, v, cb, cc, cx = jnp.split(
            proj, np.cumsum([D_ATTN, D_ATTN, D_ATTN, D_CONV, D_CONV]).tolist(), axis=-1)
        to_heads = lambda t: t.reshape(bsz, seq, N_ATTN_HEADS, HEAD_DIM).transpose(0, 2, 1, 3)
        attn = _stick_breaking_attention(to_heads(q), to_heads(k), to_heads(v))
        attn = attn.transpose(0, 2, 1, 3).reshape(bsz, seq, D_ATTN)
        conv = cb * _causal_depthwise_conv(cc * cx, conv_w[l])
        mix = jnp.concatenate([attn, conv], axis=-1) @ w_out[l]
        x = _layer_norm(DEEPNORM_ALPHA * x + gate1 * mix, ln1_g[l], ln1_b[l])

        h2 = (x * (1.0 + scale2) + shift2).reshape(bsz * seq, d)
        moe = _hierarchical_moe(h2, w_router_group[l], b_router_group[l],
                                w_router_expert[l], b_router_expert[l],
                                w_gate[l], w_up[l], w_down[l]).reshape(bsz, seq, d)
        x = _layer_norm(DEEPNORM_ALPHA * x + gate2 * moe, ln2_g[l], ln2_b[l])
    return x
```

```python
import functools

import jax
import jax.numpy as jnp
from jax import lax
from jax.experimental import pallas as pl
from jax.experimental.pallas import tpu as pltpu

F32 = jnp.float32
BF16 = jnp.bfloat16
I32 = jnp.int32
U32 = jnp.uint32

D_ATTN = 512
HEAD_DIM = 64
D_CONV = 512
CONV_WIDTH = 3
N_GROUPS = 4
EXPERTS_PER_GROUP = 8
N_EXPERTS = N_GROUPS * EXPERTS_PER_GROUP
D_EXPERT = 256
N_ADA = 6
DEPTH = 1
DEEPNORM_ALPHA = (2.0 * DEPTH) ** 0.25
LN_EPS = 1e-5

LANES = 128
SUBLANES = 8

SEQ_TILE = 512
ATTN_BLOCK = 256
ROW_TILE = 256
POS_CHUNK = 256
ROUTER_ROWS = 40
VMEM_LIMIT = 48 * 1024 * 1024

_NT = (((1,), (1,)), ((), ()))


def _dot(a, b):
    return jnp.dot(a, b, preferred_element_type=F32)


def _dot_nt(a, b):
    return lax.dot_general(a, b, _NT, preferred_element_type=F32)


def _split_bf16(x):
    hi = x.astype(BF16)
    lo = (x - hi.astype(F32)).astype(BF16)
    return hi, lo


def _layer_norm(y, g, b):
    mu = jnp.mean(y, axis=-1, keepdims=True)
    yc = y - mu
    var = jnp.mean(yc * yc, axis=-1, keepdims=True)
    return yc * lax.rsqrt(var + LN_EPS) * g + b


def _pack_bf16_pairs(x):
    n = x.shape[1] // 2
    bits = pltpu.bitcast(x.astype(BF16).astype(F32), U32)
    return (bits[:, :n] >> 16) | (bits[:, n:] & jnp.uint32(0xFFFF0000))


def _unpack_bf16_pairs(u):
    lo = pltpu.bitcast(u << 16, F32)
    hi = pltpu.bitcast(u & jnp.uint32(0xFFFF0000), F32)
    return jnp.concatenate([lo, hi], axis=1)


def _ada_kernel(c_ref, w_ref, b_ref, o_ref):
    c = c_ref[...]
    s = c * (1.0 / (1.0 + jnp.exp(-c)))
    s_hi, s_lo = _split_bf16(s)
    w_hi, w_lo = _split_bf16(w_ref[...])
    o_ref[...] = _dot(s_hi, w_hi) + _dot(s_hi, w_lo) + _dot(s_lo, w_hi) + b_ref[...]


def _ada(c8, w, b):
    d = c8.shape[1]
    n = w.shape[1]
    return pl.pallas_call(
        _ada_kernel,
        grid=(n // d,),
        in_specs=[pl.BlockSpec((SUBLANES, d), lambda j: (0, 0)),
                  pl.BlockSpec((d, d), lambda j: (0, j)),
                  pl.BlockSpec((1, d), lambda j: (0, j))],
        out_specs=pl.BlockSpec((SUBLANES, d), lambda j: (0, j)),
        out_shape=jax.ShapeDtypeStruct((SUBLANES, n), F32),
        name="ada",
    )(c8, w, b)


def _inproj_kernel(x_ref, sc_ref, sh_ref, wqk_ref, wvt_ref, wc_ref, cw_ref,
                   q_ref, k_ref, vt_ref, conv_ref, carry_ref):
    ts = x_ref.shape[1]

    @pl.when(pl.program_id(1) == 0)
    def _():
        carry_ref[...] = jnp.zeros_like(carry_ref)

    h = (x_ref[0] * (1.0 + sc_ref[0]) + sh_ref[0]).astype(BF16)
    qk = _dot(h, wqk_ref[...])
    q_ref[0] = qk[:, :D_ATTN].astype(BF16)
    k_ref[0] = qk[:, D_ATTN:].astype(BF16)
    vt_ref[0] = _dot_nt(wvt_ref[...], h).astype(BF16)

    c3 = _dot(h, wc_ref[...])
    cb = c3[:, :D_CONV]
    u = c3[:, D_CONV:2 * D_CONV] * c3[:, 2 * D_CONV:]
    prev = carry_ref[...]
    row = lax.broadcasted_iota(I32, u.shape, 0)
    u1 = jnp.where(row == 0, prev[SUBLANES - 1:SUBLANES, :], pltpu.roll(u, 1, 0))
    u2 = jnp.where(row == 0, prev[SUBLANES - 2:SUBLANES - 1, :],
                   jnp.where(row == 1, prev[SUBLANES - 1:SUBLANES, :], pltpu.roll(u, 2, 0)))
    cw = cw_ref[...]
    conv = cb * (cw[0:1, :] * u2 + cw[1:2, :] * u1 + cw[2:3, :] * u)
    conv_ref[0] = conv.astype(BF16)
    carry_ref[...] = u[ts - SUBLANES:, :]


def _inproj(x, scale1, shift1, wqk, wvt, wc, conv_w):
    b, s, d = x.shape
    ts = min(SEQ_TILE, s)
    vec = pl.BlockSpec((1, 1, d), lambda i, j: (i, 0, 0))
    full = lambda a: pl.BlockSpec(a.shape, lambda i, j: (0,) * a.ndim)
    return pl.pallas_call(
        _inproj_kernel,
        grid=(b, s // ts),
        in_specs=[pl.BlockSpec((1, ts, d), lambda i, j: (i, j, 0)), vec, vec,
                  full(wqk), full(wvt), full(wc), full(conv_w)],
        out_specs=[pl.BlockSpec((1, ts, D_ATTN), lambda i, j: (i, j, 0)),
                   pl.BlockSpec((1, ts, D_ATTN), lambda i, j: (i, j, 0)),
                   pl.BlockSpec((1, D_ATTN, ts), lambda i, j: (i, 0, j)),
                   pl.BlockSpec((1, ts, D_CONV), lambda i, j: (i, j, 0))],
        out_shape=[jax.ShapeDtypeStruct((b, s, D_ATTN), BF16),
                   jax.ShapeDtypeStruct((b, s, D_ATTN), BF16),
                   jax.ShapeDtypeStruct((b, D_ATTN, s), BF16),
                   jax.ShapeDtypeStruct((b, s, D_CONV), BF16)],
        scratch_shapes=[pltpu.VMEM((SUBLANES, D_CONV), F32)],
        compiler_params=pltpu.CompilerParams(
            dimension_semantics=("arbitrary", "arbitrary"), vmem_limit_bytes=VMEM_LIMIT),
        name="inproj",
    )(x, scale1, shift1, wqk, wvt, wc, conv_w)


def _attn_tile(qh, kj, vtj, mt, carry, acc, causal):
    z = _dot_nt(kj, qh)
    log_stay = -(jnp.maximum(z, 0.0) + jnp.log(1.0 + jnp.exp(-jnp.abs(z))))
    if causal is not None:
        log_stay = jnp.where(causal, log_stay, 0.0)
    ls16 = log_stay.astype(BF16)
    after = _dot(mt, ls16)
    a = jnp.exp((z + log_stay) + (after + carry))
    if causal is not None:
        a = jnp.where(causal, a, 0.0)
    acc = acc + _dot(vtj, a.astype(BF16))
    carry = carry + (after[0:1, :] + ls16[0:1, :].astype(F32))
    return carry, acc


def _attn_kernel(q_ref, k_ref, vt_ref, mt_ref, o_ref):
    s = q_ref.shape[1]
    blk = mt_ref.shape[0]
    nq = s // blk
    lane = lax.broadcasted_iota(I32, (blk, 2 * HEAD_DIM), 1)
    row = lax.broadcasted_iota(I32, (blk, blk), 0)
    col = lax.broadcasted_iota(I32, (blk, blk), 1)
    causal = row < col

    def q_block(qi, _):
        q0 = pl.multiple_of(qi * blk, blk)
        q2 = q_ref[0, pl.ds(q0, blk), :]
        qa = jnp.where(lane < HEAD_DIM, q2, jnp.zeros_like(q2))
        qb = jnp.where(lane >= HEAD_DIM, q2, jnp.zeros_like(q2))

        def pair(j, state, mask):
            ca, aa, cb, ab = state
            k0 = pl.multiple_of(j * blk, blk)
            kj = k_ref[0, pl.ds(k0, blk), :]
            vtj = vt_ref[0, :, pl.ds(k0, blk)]
            mt = mt_ref[...]
            ca, aa = _attn_tile(qa, kj, vtj[:HEAD_DIM], mt, ca, aa, mask)
            cb, ab = _attn_tile(qb, kj, vtj[HEAD_DIM:], mt, cb, ab, mask)
            return ca, aa, cb, ab

        zc = jnp.zeros((1, blk), F32)
        za = jnp.zeros((HEAD_DIM, blk), F32)
        state = pair(qi, (zc, za, zc, za), causal)
        state = lax.fori_loop(0, qi, lambda t, st: pair(qi - 1 - t, st, None), state)
        out_t = jnp.concatenate([state[1], state[3]], axis=0)
        o_ref[0, pl.ds(q0, blk), :] = out_t.T.astype(BF16)
        return 0

    lax.fori_loop(0, nq, q_block, 0)


def _attn(q, k, vt, mt):
    b, s, _ = q.shape
    pairs = D_ATTN // (2 * HEAD_DIM)
    return pl.pallas_call(
        _attn_kernel,
        grid=(b, pairs),
        in_specs=[pl.BlockSpec((1, s, 2 * HEAD_DIM), lambda i, j: (i, 0, j)),
                  pl.BlockSpec((1, s, 2 * HEAD_DIM), lambda i, j: (i, 0, j)),
                  pl.BlockSpec((1, 2 * HEAD_DIM, s), lambda i, j: (i, j, 0)),
                  pl.BlockSpec(mt.shape, lambda i, j: (0, 0))],
        out_specs=pl.BlockSpec((1, s, 2 * HEAD_DIM), lambda i, j: (i, 0, j)),
        out_shape=jax.ShapeDtypeStruct((b, s, D_ATTN), BF16),
        compiler_params=pltpu.CompilerParams(
            dimension_semantics=("arbitrary", "arbitrary"), vmem_limit_bytes=VMEM_LIMIT),
        name="attn",
    )(q, k, vt, mt)


def _outproj_kernel(attn_ref, conv_ref, x_ref, g1_ref, sc_ref, sh_ref, lg_ref, lb_ref,
                    wo_ref, wrh_ref, wrl_ref, rb_ref,
                    x1_ref, h2p_ref, eid_ref, wcol_ref):
    ts = x_ref.shape[1]
    cat = jnp.concatenate([attn_ref[0], conv_ref[0]], axis=-1)
    mix = _dot(cat, wo_ref[...])
    x1 = _layer_norm(DEEPNORM_ALPHA * x_ref[0] + g1_ref[0] * mix, lg_ref[...], lb_ref[...])
    x1_ref[0] = x1
    h2 = x1 * (1.0 + sc_ref[0]) + sh_ref[0]
    h2p_ref[0] = _pack_bf16_pairs(h2)

    h_hi, h_lo = _split_bf16(h2)
    wrh = wrh_ref[...]
    logit = _dot_nt(wrh, h_hi) + _dot_nt(wrl_ref[...], h_hi) + _dot_nt(wrh, h_lo) + rb_ref[...]

    e = EXPERTS_PER_GROUP
    rio = lax.broadcasted_iota(I32, (e, ts), 0)
    first_where = lambda hit: jnp.min(jnp.where(hit, rio, e), axis=0, keepdims=True)

    gl = logit[N_EXPERTS:N_EXPERTS + e]
    gmax = jnp.max(gl, axis=0, keepdims=True)
    g_prob = 1.0 / jnp.sum(jnp.exp(gl - gmax), axis=0, keepdims=True)
    g_idx = first_where(gl == gmax)
    sel = logit[0:e]
    for g in range(1, N_GROUPS):
        sel = jnp.where(g_idx == g, logit[g * e:(g + 1) * e], sel)
    m1 = jnp.max(sel, axis=0, keepdims=True)
    i1 = first_where(sel == m1)
    sel2 = jnp.where(rio == i1, -jnp.inf, sel)
    m2 = jnp.max(sel2, axis=0, keepdims=True)
    i2 = first_where(sel2 == m2)
    r = jnp.exp(m2 - m1)
    w1 = g_prob / (1.0 + r)
    w2 = w1 * r
    eid_ref[0:1, :] = g_idx * e + i1
    eid_ref[1:2, :] = g_idx * e + i2
    wrows = jnp.concatenate([w1, w2, jnp.zeros((LANES - 2, ts), F32)], axis=0)
    wcol_ref[...] = wrows.T


def _outproj(attn, conv, x, gate1, scale2, shift2, ln_g, ln_b, wo, wr_hi, wr_lo, rb):
    b, s, d = x.shape
    ts = min(SEQ_TILE, s)
    nj = s // ts
    vec = pl.BlockSpec((1, 1, d), lambda i, j: (i, 0, 0))
    full = lambda a: pl.BlockSpec(a.shape, lambda i, j: (0,) * a.ndim)
    tile = lambda w: pl.BlockSpec((1, ts, w), lambda i, j: (i, j, 0))
    return pl.pallas_call(
        _outproj_kernel,
        grid=(b, nj),
        in_specs=[tile(D_ATTN), tile(D_CONV), tile(d), vec, vec, vec, full(ln_g), full(ln_b),
                  full(wo), full(wr_hi), full(wr_lo), full(rb)],
        out_specs=[tile(d), tile(d // 2),
                   pl.BlockSpec((2, ts), lambda i, j: (0, i * nj + j)),
                   pl.BlockSpec((ts, LANES), lambda i, j: (i * nj + j, 0))],
        out_shape=[jax.ShapeDtypeStruct((b, s, d), F32),
                   jax.ShapeDtypeStruct((b, s, d // 2), U32),
                   jax.ShapeDtypeStruct((2, b * s), I32),
                   jax.ShapeDtypeStruct((b * s, LANES), F32)],
        compiler_params=pltpu.CompilerParams(
            dimension_semantics=("arbitrary", "arbitrary"), vmem_limit_bytes=VMEM_LIMIT),
        name="outproj",
    )(attn, conv, x, gate1, scale2, shift2, ln_g, ln_b, wo, wr_hi, wr_lo, rb)


def _pos_kernel(eid_ref, pos_ref, te_ref, *, row_tile):
    nchunk, ch = eid_ref.shape
    eio = lax.broadcasted_iota(I32, (N_EXPERTS, ch), 0)
    ones = jnp.ones((ch, LANES), BF16)
    tile_lanes = te_ref.shape[1]

    def onehot(c):
        return eid_ref[pl.ds(c, 1), :] == eio

    cnt = lax.fori_loop(0, nchunk, lambda c, acc: acc + _dot(onehot(c).astype(BF16), ones),
                        jnp.zeros((N_EXPERTS, LANES), F32))
    ntile = jnp.floor((cnt + (row_tile - 1)) * (1.0 / row_tile))
    lower = (lax.broadcasted_iota(I32, (N_EXPERTS, N_EXPERTS), 1)
             < lax.broadcasted_iota(I32, (N_EXPERTS, N_EXPERTS), 0)).astype(BF16)
    first_tile = _dot(lower, ntile.astype(BF16))
    end_tile = first_tile + ntile

    tio = lax.broadcasted_iota(I32, (N_EXPERTS, tile_lanes), 1).astype(F32)
    ends = jnp.concatenate([end_tile] * (tile_lanes // LANES), axis=1)
    owner = jnp.sum((ends <= tio).astype(F32), axis=0, keepdims=True)
    te_ref[...] = jnp.minimum(owner, N_EXPERTS - 1).astype(I32)

    before = (lax.broadcasted_iota(I32, (ch, ch), 0)
              < lax.broadcasted_iota(I32, (ch, ch), 1)).astype(BF16)

    def place(c, run):
        hit = onehot(c)
        hit16 = hit.astype(BF16)
        rank = _dot(hit16, before)
        base = jnp.concatenate([run] * (ch // LANES), axis=1)
        p = jnp.sum(jnp.where(hit, rank + base, 0.0), axis=0, keepdims=True)
        pos_ref[pl.ds(c, 1), :] = p.astype(I32)
        return run + _dot(hit16, ones)

    lax.fori_loop(0, nchunk, place, first_tile * row_tile)


def _positions(eid, n_tiles):
    npairs = eid.shape[0] * eid.shape[1]
    nchunk = npairs // POS_CHUNK
    tile_lanes = pl.cdiv(n_tiles, LANES) * LANES
    pos, te = pl.pallas_call(
        functools.partial(_pos_kernel, row_tile=ROW_TILE),
        out_shape=[jax.ShapeDtypeStruct((nchunk, POS_CHUNK), I32),
                   jax.ShapeDtypeStruct((1, tile_lanes), I32)],
        name="pos",
    )(eid.reshape(nchunk, POS_CHUNK))
    return pos.reshape(eid.shape), te.reshape(tile_lanes)


def _dispatch_kernel(pos_ref, src_hbm, zero_hbm, dst_hbm, sem):
    del zero_hbm
    tm = pos_ref.shape[1]
    t0 = pl.program_id(0) * tm

    def row_copy(slot, r):
        return pltpu.make_async_copy(src_hbm.at[pl.ds(t0 + r, 1), :],
                                     dst_hbm.at[pl.ds(pos_ref[slot, r], 1), :], sem)

    def issue(r, _):
        row_copy(0, r).start()
        row_copy(1, r).start()
        return 0

    def drain(r, _):
        row_copy(0, r).wait()
        row_copy(1, r).wait()
        return 0

    lax.fori_loop(0, tm, issue, 0, unroll=8)
    lax.fori_loop(0, tm, drain, 0, unroll=8)


def _dispatch(pos, h2p, n_rows):
    t, w = h2p.shape
    tm = min(ROW_TILE, t)
    zeros = jnp.zeros((n_rows, w), h2p.dtype)
    return pl.pallas_call(
        _dispatch_kernel,
        grid=(t // tm,),
        in_specs=[pl.BlockSpec((2, tm), lambda i: (0, i), memory_space=pltpu.SMEM),
                  pl.BlockSpec(memory_space=pl.ANY),
                  pl.BlockSpec(memory_space=pl.ANY)],
        out_specs=pl.BlockSpec(memory_space=pl.ANY),
        out_shape=jax.ShapeDtypeStruct((n_rows, w), h2p.dtype),
        scratch_shapes=[pltpu.SemaphoreType.DMA],
        input_output_aliases={2: 0},
        compiler_params=pltpu.CompilerParams(dimension_semantics=("arbitrary",)),
        name="dispatch",
    )(pos, h2p, zeros)


def _expert_kernel(te_ref, xs_ref, wgu_ref, wd_ref, ys_ref):
    del te_ref
    x = _unpack_bf16_pairs(xs_ref[...]).astype(BF16)
    gu = _dot(x, wgu_ref[0])
    gate = gu[:, :D_EXPERT]
    hid = gate * (1.0 / (1.0 + jnp.exp(-gate))) * gu[:, D_EXPERT:]
    ys_ref[...] = _pack_bf16_pairs(_dot(hid.astype(BF16), wd_ref[0]))


def _experts(te, xs, wgu, wd):
    n_rows, w = xs.shape
    tm = ROW_TILE
    d = 2 * w
    return pl.pallas_call(
        _expert_kernel,
        grid_spec=pltpu.PrefetchScalarGridSpec(
            num_scalar_prefetch=1,
            grid=(n_rows // tm,),
            in_specs=[pl.BlockSpec((tm, w), lambda t, te: (t, 0)),
                      pl.BlockSpec((1, d, 2 * D_EXPERT), lambda t, te: (te[t], 0, 0)),
                      pl.BlockSpec((1, D_EXPERT, d), lambda t, te: (te[t], 0, 0))],
            out_specs=pl.BlockSpec((tm, w), lambda t, te: (t, 0))),
        out_shape=jax.ShapeDtypeStruct((n_rows, w), U32),
        compiler_params=pltpu.CompilerParams(
            dimension_semantics=("arbitrary",), vmem_limit_bytes=VMEM_LIMIT),
        name="expert",
    )(te, xs, wgu, wd)


def _combine_kernel(pos_ref, posn_ref, x1_ref, wcol_ref, g2_ref, lg_ref, lb_ref, ys_hbm,
                    o_ref, gbuf, sem):
    tm = pos_ref.shape[1]
    i = pl.program_id(0)
    n = pl.num_programs(0)
    cur = lax.rem(i, 2)

    def row_copy(p_ref, slot, r, buf):
        return pltpu.make_async_copy(ys_hbm.at[pl.ds(p_ref[slot, r], 1), :],
                                     gbuf.at[buf, slot, pl.ds(r, 1), :], sem.at[buf])

    def issue(p_ref, buf):
        def body(r, _):
            row_copy(p_ref, 0, r, buf).start()
            row_copy(p_ref, 1, r, buf).start()
            return 0
        lax.fori_loop(0, tm, body, 0, unroll=8)

    @pl.when(i == 0)
    def _():
        issue(pos_ref, 0)

    @pl.when(i + 1 < n)
    def _():
        issue(posn_ref, 1 - cur)

    def drain(r, _):
        row_copy(pos_ref, 0, r, cur).wait()
        row_copy(pos_ref, 1, r, cur).wait()
        return 0
    lax.fori_loop(0, tm, drain, 0, unroll=8)

    y0 = _unpack_bf16_pairs(gbuf[cur, 0])
    y1 = _unpack_bf16_pairs(gbuf[cur, 1])
    wc = wcol_ref[...]
    moe = wc[:, 0:1] * y0 + wc[:, 1:2] * y1
    o_ref[...] = _layer_norm(DEEPNORM_ALPHA * x1_ref[...] + g2_ref[0] * moe, lg_ref[...], lb_ref[...])


def _combine(pos, x1, wcol, gate2, ln_g, ln_b, ys, seq):
    t, d = x1.shape
    tm = min(ROW_TILE, seq)
    n = t // tm
    per_seq = seq // tm
    full = lambda a: pl.BlockSpec(a.shape, lambda i: (0,) * a.ndim)
    return pl.pallas_call(
        _combine_kernel,
        grid=(n,),
        in_specs=[pl.BlockSpec((2, tm), lambda i: (0, i), memory_space=pltpu.SMEM),
                  pl.BlockSpec((2, tm), lambda i: (0, jnp.minimum(i + 1, n - 1)), memory_space=pltpu.SMEM),
                  pl.BlockSpec((tm, d), lambda i: (i, 0)),
                  pl.BlockSpec((tm, LANES), lambda i: (i, 0)),
                  pl.BlockSpec((1, 1, d), lambda i: (i // per_seq, 0, 0)),
                  full(ln_g), full(ln_b),
                  pl.BlockSpec(memory_space=pl.ANY)],
        out_specs=pl.BlockSpec((tm, d), lambda i: (i, 0)),
        out_shape=jax.ShapeDtypeStruct((t, d), F32),
        scratch_shapes=[pltpu.VMEM((2, 2, tm, d // 2), U32), pltpu.SemaphoreType.DMA((2,))],
        compiler_params=pltpu.CompilerParams(
            dimension_semantics=("arbitrary",), vmem_limit_bytes=VMEM_LIMIT),
        name="combine",
    )(pos, pos, x1, wcol, gate2, ln_g, ln_b, ys)


def kernel(x, c, w_ada, b_ada, w_in, conv_w, w_out, ln1_g, ln1_b, w_router_group, b_router_group,
           w_router_expert, b_router_expert, w_gate, w_up, w_down, ln2_g, ln2_b):
    b, s, d = x.shape
    t = b * s
    assert d == D_ATTN + D_CONV and w_ada.shape[0] == DEPTH
    assert s % ATTN_BLOCK == 0 and s % min(SEQ_TILE, s) == 0 and (2 * t) % POS_CHUNK == 0 and b <= SUBLANES

    c8 = jnp.zeros((SUBLANES, d), F32).at[:b].set(c)
    ada = _ada(c8, w_ada[0], b_ada[0][None, :])[:b].reshape(b, N_ADA, 1, d)
    shift1, scale1, gate1, shift2, scale2, gate2 = [ada[:, i] for i in range(N_ADA)]

    wi = w_in[0]
    wqk = jnp.concatenate([wi[:, :D_ATTN] * (HEAD_DIM ** -0.5), wi[:, D_ATTN:2 * D_ATTN]], axis=1).astype(BF16)
    wvt = wi[:, 2 * D_ATTN:3 * D_ATTN].T.astype(BF16)
    wc = wi[:, 3 * D_ATTN:].astype(BF16)
    wo = w_out[0].astype(BF16)
    wr = jnp.concatenate([w_router_expert[0].T, w_router_group[0].T,
                          jnp.zeros((ROUTER_ROWS - N_EXPERTS - N_GROUPS, d), F32)], axis=0)
    wr_hi = wr.astype(BF16)
    wr_lo = (wr - wr_hi.astype(F32)).astype(BF16)
    rb = jnp.concatenate([b_router_expert[0], b_router_group[0],
                          jnp.full((ROUTER_ROWS - N_EXPERTS - N_GROUPS,), -1e30, F32)])[:, None]
    wgu = jnp.concatenate([w_gate[0], w_up[0]], axis=-1).astype(BF16)
    wd = w_down[0].astype(BF16)
    mt = jnp.triu(jnp.ones((ATTN_BLOCK, ATTN_BLOCK), BF16), 1)

    q, k, vt, conv = _inproj(x, scale1, shift1, wqk, wvt, wc, conv_w[0])
    attn = _attn(q, k, vt, mt)
    x1, h2p, eid, wcol = _outproj(attn, conv, x, gate1, scale2, shift2, ln1_g, ln1_b,
                                  wo, wr_hi, wr_lo, rb)

    n_tiles = (2 * t) // ROW_TILE + N_EXPERTS
    pos, te = _positions(eid, n_tiles)
    xs = _dispatch(pos, h2p.reshape(t, d // 2), n_tiles * ROW_TILE)
    ys = _experts(te, xs, wgu, wd)
    out = _combine(pos, x1.reshape(t, d), wcol, gate2, ln2_g, ln2_b, ys, s)
    return out.reshape(b, s, d)
```

```python
import functools
import math

import jax
import jax.numpy as jnp
from jax import lax
from jax.experimental import pallas as pl
from jax.experimental.pallas import tpu as pltpu

F32 = jnp.float32
BF16 = jnp.bfloat16
I32 = jnp.int32

D_ATTN = 512
HEAD_DIM = 64
D_CONV = 512
CONV_WIDTH = 3
N_GROUPS = 4
EXPERTS_PER_GROUP = 8
N_EXPERTS = N_GROUPS * EXPERTS_PER_GROUP
D_EXPERT = 256
N_ADA = 6
DEPTH = 1
DEEPNORM_ALPHA = (2.0 * DEPTH) ** 0.25
LN_EPS = 1e-5
LOG2E = math.log2(math.e)

LANES = 128
SUBLANES = 8

SEQ_TILE = 512
KEY_BLOCK = 256
QUERY_BLOCK = 512
PAIRS_PER_STEP = 2
ROW_TILE = 256
DISPATCH_TILE = 512
POS_CHUNK = 256
ROUTER_ROWS = 40
VMEM_LIMIT = 48 * 1024 * 1024

_NT = (((1,), (1,)), ((), ()))


def _dot(a, b):
    return jnp.dot(a, b, preferred_element_type=F32)


def _dot_nt(a, b):
    return lax.dot_general(a, b, _NT, preferred_element_type=F32)


def _split_bf16(x):
    hi = x.astype(BF16)
    lo = (x - hi.astype(F32)).astype(BF16)
    return hi, lo


def _layer_norm(y, g, b):
    mu = jnp.mean(y, axis=-1, keepdims=True)
    yc = y - mu
    var = jnp.mean(yc * yc, axis=-1, keepdims=True)
    return yc * lax.rsqrt(var + LN_EPS) * g + b


def _ada_kernel(c_ref, w_ref, b_ref, o_ref):
    c = c_ref[...]
    s = c * (1.0 / (1.0 + jnp.exp(-c)))
    s_hi, s_lo = _split_bf16(s)
    w_hi, w_lo = _split_bf16(w_ref[...])
    o_ref[...] = _dot(s_hi, w_hi) + _dot(s_hi, w_lo) + _dot(s_lo, w_hi) + b_ref[...]


def _ada(c8, w, b):
    d = c8.shape[1]
    n = w.shape[1]
    return pl.pallas_call(
        _ada_kernel,
        grid=(n // d,),
        in_specs=[pl.BlockSpec((SUBLANES, d), lambda j: (0, 0)),
                  pl.BlockSpec((d, d), lambda j: (0, j)),
                  pl.BlockSpec((1, d), lambda j: (0, j))],
        out_specs=pl.BlockSpec((SUBLANES, d), lambda j: (0, j)),
        out_shape=jax.ShapeDtypeStruct((SUBLANES, n), F32),
        name="ada",
    )(c8, w, b)


def _inproj_kernel(x_ref, sc_ref, sh_ref, wqkv_ref, wc_ref, cw_ref,
                   q_ref, k_ref, v_ref, conv_ref, carry_ref):
    ts = x_ref.shape[1]

    @pl.when(pl.program_id(1) == 0)
    def _():
        carry_ref[...] = jnp.zeros_like(carry_ref)

    h = (x_ref[0] * (1.0 + sc_ref[0]) + sh_ref[0]).astype(BF16)
    qkv = _dot(h, wqkv_ref[...])
    q_ref[0] = qkv[:, :D_ATTN].astype(BF16)
    k_ref[0] = qkv[:, D_ATTN:2 * D_ATTN].astype(BF16)
    v_ref[0] = qkv[:, 2 * D_ATTN:].astype(BF16)

    c3 = _dot(h, wc_ref[...])
    cb = c3[:, :D_CONV]
    u = c3[:, D_CONV:2 * D_CONV] * c3[:, 2 * D_CONV:]
    prev = carry_ref[...]
    row = lax.broadcasted_iota(I32, u.shape, 0)
    u1 = jnp.where(row == 0, prev[SUBLANES - 1:SUBLANES, :], pltpu.roll(u, 1, 0))
    u2 = jnp.where(row == 0, prev[SUBLANES - 2:SUBLANES - 1, :],
                   jnp.where(row == 1, prev[SUBLANES - 1:SUBLANES, :], pltpu.roll(u, 2, 0)))
    cw = cw_ref[...]
    conv = cb * (cw[0:1, :] * u2 + cw[1:2, :] * u1 + cw[2:3, :] * u)
    conv_ref[0] = conv.astype(BF16)
    carry_ref[...] = u[ts - SUBLANES:, :]


def _inproj(x, scale1, shift1, wqkv, wc, conv_w):
    b, s, d = x.shape
    ts = min(SEQ_TILE, s)
    vec = pl.BlockSpec((1, 1, d), lambda i, j: (i, 0, 0))
    full = lambda a: pl.BlockSpec(a.shape, lambda i, j: (0,) * a.ndim)
    tile = lambda w: pl.BlockSpec((1, ts, w), lambda i, j: (i, j, 0))
    act = lambda w: jax.ShapeDtypeStruct((b, s, w), BF16)
    return pl.pallas_call(
        _inproj_kernel,
        grid=(b, s // ts),
        in_specs=[tile(d), vec, vec, full(wqkv), full(wc), full(conv_w)],
        out_specs=[tile(D_ATTN), tile(D_ATTN), tile(D_ATTN), tile(D_CONV)],
        out_shape=[act(D_ATTN), act(D_ATTN), act(D_ATTN), act(D_CONV)],
        scratch_shapes=[pltpu.VMEM((SUBLANES, D_CONV), F32)],
        compiler_params=pltpu.CompilerParams(
            dimension_semantics=("arbitrary", "arbitrary"), vmem_limit_bytes=VMEM_LIMIT),
        name="inproj",
    )(x, scale1, shift1, wqkv, wc, conv_w)


def _attn_tile(qs, kj, vj, tri, carry, acc, causal):
    z = _dot_nt(qs, kj)
    sp = jnp.maximum(z, 0.0) + jnp.log(1.0 + jnp.exp2(-jnp.abs(z))) * LOG2E
    if causal is not None:
        sp = jnp.where(causal, sp, 0.0)
    sp16 = sp.astype(BF16)
    after = _dot(sp16, tri)
    a = jnp.exp2(z - (sp + (after + carry)))
    if causal is not None:
        a = jnp.where(causal, a, 0.0)
    acc = acc + _dot(a.astype(BF16), vj)
    carry = carry + (after[:, 0:1] + sp16[:, 0:1].astype(F32))
    return carry, acc


def _attn_kernel(q_ref, k_ref, v_ref, tri_ref, o_ref):
    s = q_ref.shape[1]
    bk = tri_ref.shape[0]
    bq = min(QUERY_BLOCK, s)
    per_q = bq // bk
    rows = 2 * bq
    pw = 2 * HEAD_DIM
    npairs = q_ref.shape[2] // pw
    lane = lax.broadcasted_iota(I32, (bq, pw), 1)
    qrow = lax.broadcasted_iota(I32, (rows, bk), 0) & (bq - 1)
    kcol = lax.broadcasted_iota(I32, (rows, bk), 1)

    def q_block(qi, _):
        q0 = pl.multiple_of(qi * bq, bq)
        qs = []
        for p in range(npairs):
            q2 = q_ref[0, pl.ds(q0, bq), p * pw:(p + 1) * pw]
            zero = jnp.zeros_like(q2)
            qs.append(jnp.concatenate([jnp.where(lane < HEAD_DIM, q2, zero),
                                       jnp.where(lane >= HEAD_DIM, q2, zero)], axis=0))

        def tiles(j, state, causal):
            k0 = pl.multiple_of(j * bk, bk)
            out = []
            for p in range(npairs):
                out.extend(_attn_tile(qs[p], k_ref[0, pl.ds(k0, bk), p * pw:(p + 1) * pw],
                                      v_ref[0, pl.ds(k0, bk), p * pw:(p + 1) * pw],
                                      tri_ref[...], state[2 * p], state[2 * p + 1], causal))
            return tuple(out)

        state = (jnp.zeros((rows, 1), F32), jnp.zeros((rows, pw), F32)) * npairs
        for d in reversed(range(per_q)):
            state = tiles(qi * per_q + d, state, kcol + d * bk < qrow)
        def visible(t, st):
            for d in range(per_q):
                st = tiles((qi - t) * per_q - 1 - d, st, None)
            return st
        state = lax.fori_loop(0, qi, visible, state)
        for p in range(npairs):
            acc = state[2 * p + 1]
            o_ref[0, pl.ds(q0, bq), p * pw:(p + 1) * pw] = (
                jnp.where(lane < HEAD_DIM, acc[:bq], acc[bq:]).astype(BF16))
        return 0

    lax.fori_loop(0, s // bq, q_block, 0)


def _attn(q, k, v, tri):
    b, s, _ = q.shape
    width = PAIRS_PER_STEP * 2 * HEAD_DIM
    spec = pl.BlockSpec((1, s, width), lambda i, j: (i, 0, j))
    return pl.pallas_call(
        _attn_kernel,
        grid=(b, D_ATTN // width),
        in_specs=[spec, spec, spec, pl.BlockSpec(tri.shape, lambda i, j: (0, 0))],
        out_specs=spec,
        out_shape=jax.ShapeDtypeStruct((b, s, D_ATTN), BF16),
        compiler_params=pltpu.CompilerParams(
            dimension_semantics=("arbitrary", "arbitrary"), vmem_limit_bytes=VMEM_LIMIT),
        name="attn",
    )(q, k, v, tri)


def _outproj_kernel(attn_ref, conv_ref, x_ref, g1_ref, sc_ref, sh_ref, lg_ref, lb_ref,
                    wo_ref, wrh_ref, wrl_ref, rb_ref,
                    x1_ref, h2_ref, eid_ref, wcol_ref):
    ts = x_ref.shape[1]
    cat = jnp.concatenate([attn_ref[0], conv_ref[0]], axis=-1)
    mix = _dot(cat, wo_ref[...])
    x1 = _layer_norm(DEEPNORM_ALPHA * x_ref[0] + g1_ref[0] * mix, lg_ref[...], lb_ref[...])
    x1_ref[0] = x1
    h2 = x1 * (1.0 + sc_ref[0]) + sh_ref[0]
    h2_ref[0] = h2

    h_hi, h_lo = _split_bf16(h2)
    wrh = wrh_ref[...]
    logit = _dot_nt(wrh, h_hi) + _dot_nt(wrl_ref[...], h_hi) + _dot_nt(wrh, h_lo) + rb_ref[...]

    e = EXPERTS_PER_GROUP
    rio = lax.broadcasted_iota(I32, (e, ts), 0)
    first_where = lambda hit: jnp.min(jnp.where(hit, rio, e), axis=0, keepdims=True)

    gl = logit[N_EXPERTS:N_EXPERTS + e]
    gmax = jnp.max(gl, axis=0, keepdims=True)
    g_prob = 1.0 / jnp.sum(jnp.exp(gl - gmax), axis=0, keepdims=True)
    g_idx = first_where(gl == gmax)
    sel = logit[0:e]
    for g in range(1, N_GROUPS):
        sel = jnp.where(g_idx == g, logit[g * e:(g + 1) * e], sel)
    m1 = jnp.max(sel, axis=0, keepdims=True)
    i1 = first_where(sel == m1)
    sel2 = jnp.where(rio == i1, -jnp.inf, sel)
    m2 = jnp.max(sel2, axis=0, keepdims=True)
    i2 = first_where(sel2 == m2)
    r = jnp.exp(m2 - m1)
    w1 = g_prob / (1.0 + r)
    w2 = w1 * r
    eid_ref[0:1, :] = g_idx * e + i1
    eid_ref[1:2, :] = g_idx * e + i2
    wrows = jnp.concatenate([w1, w2, jnp.zeros((LANES - 2, ts), F32)], axis=0)
    wcol_ref[...] = wrows.T


def _outproj(attn, conv, x, gate1, scale2, shift2, ln_g, ln_b, wo, wr_hi, wr_lo, rb):
    b, s, d = x.shape
    ts = min(SEQ_TILE, s)
    nj = s // ts
    vec = pl.BlockSpec((1, 1, d), lambda i, j: (i, 0, 0))
    full = lambda a: pl.BlockSpec(a.shape, lambda i, j: (0,) * a.ndim)
    tile = lambda w: pl.BlockSpec((1, ts, w), lambda i, j: (i, j, 0))
    return pl.pallas_call(
        _outproj_kernel,
        grid=(b, nj),
        in_specs=[tile(D_ATTN), tile(D_CONV), tile(d), vec, vec, vec, full(ln_g), full(ln_b),
                  full(wo), full(wr_hi), full(wr_lo), full(rb)],
        out_specs=[tile(d), tile(d),
                   pl.BlockSpec((2, ts), lambda i, j: (0, i * nj + j)),
                   pl.BlockSpec((ts, LANES), lambda i, j: (i * nj + j, 0))],
        out_shape=[jax.ShapeDtypeStruct((b, s, d), F32),
                   jax.ShapeDtypeStruct((b, s, d), F32),
                   jax.ShapeDtypeStruct((2, b * s), I32),
                   jax.ShapeDtypeStruct((b * s, LANES), F32)],
        compiler_params=pltpu.CompilerParams(
            dimension_semantics=("arbitrary", "arbitrary"), vmem_limit_bytes=VMEM_LIMIT),
        name="outproj",
    )(attn, conv, x, gate1, scale2, shift2, ln_g, ln_b, wo, wr_hi, wr_lo, rb)


def _pos_kernel(eid_ref, pos_ref, te_ref, *, row_tile):
    nchunk, ch = eid_ref.shape
    eio = lax.broadcasted_iota(I32, (N_EXPERTS, ch), 0)
    ones = jnp.ones((ch, LANES), BF16)
    tile_lanes = te_ref.shape[1]

    def onehot(c):
        return eid_ref[pl.ds(c, 1), :] == eio

    cnt = lax.fori_loop(0, nchunk, lambda c, acc: acc + _dot(onehot(c).astype(BF16), ones),
                        jnp.zeros((N_EXPERTS, LANES), F32))
    ntile = jnp.floor((cnt + (row_tile - 1)) * (1.0 / row_tile))
    lower = (lax.broadcasted_iota(I32, (N_EXPERTS, N_EXPERTS), 1)
             < lax.broadcasted_iota(I32, (N_EXPERTS, N_EXPERTS), 0)).astype(BF16)
    first_tile = _dot(lower, ntile.astype(BF16))
    end_tile = first_tile + ntile

    tio = lax.broadcasted_iota(I32, (N_EXPERTS, tile_lanes), 1).astype(F32)
    ends = jnp.concatenate([end_tile] * (tile_lanes // LANES), axis=1)
    owner = jnp.sum((ends <= tio).astype(F32), axis=0, keepdims=True)
    te_ref[...] = jnp.minimum(owner, N_EXPERTS - 1).astype(I32)

    before = (lax.broadcasted_iota(I32, (ch, ch), 0)
              < lax.broadcasted_iota(I32, (ch, ch), 1)).astype(BF16)

    def place(c, run):
        hit = onehot(c)
        hit16 = hit.astype(BF16)
        rank = _dot(hit16, before)
        base = jnp.concatenate([run] * (ch // LANES), axis=1)
        p = jnp.sum(jnp.where(hit, rank + base, 0.0), axis=0, keepdims=True)
        pos_ref[pl.ds(c, 1), :] = p.astype(I32)
        return run + _dot(hit16, ones)

    lax.fori_loop(0, nchunk, place, first_tile * row_tile)


def _positions(eid, n_tiles):
    npairs = eid.shape[0] * eid.shape[1]
    nchunk = npairs // POS_CHUNK
    tile_lanes = pl.cdiv(n_tiles, LANES) * LANES
    pos, te = pl.pallas_call(
        functools.partial(_pos_kernel, row_tile=ROW_TILE),
        out_shape=[jax.ShapeDtypeStruct((nchunk, POS_CHUNK), I32),
                   jax.ShapeDtypeStruct((1, tile_lanes), I32)],
        name="pos",
    )(eid.reshape(nchunk, POS_CHUNK))
    return pos.reshape(eid.shape), te.reshape(tile_lanes)


def _dispatch_kernel(pos_ref, src_ref, zero_hbm, dst_hbm, sem):
    del zero_hbm
    tm = pos_ref.shape[1]

    def row_copy(slot, r):
        return pltpu.make_async_copy(src_ref.at[pl.ds(r, 1), :],
                                     dst_hbm.at[pl.ds(pos_ref[slot, r], 1), :], sem)

    def issue(r, _):
        row_copy(0, r).start()
        row_copy(1, r).start()
        return 0

    def drain(r, _):
        row_copy(0, r).wait()
        row_copy(1, r).wait()
        return 0

    lax.fori_loop(0, tm, issue, 0, unroll=8)
    lax.fori_loop(0, tm, drain, 0, unroll=8)


def _dispatch(pos, h2, n_rows):
    t, d = h2.shape
    tm = min(DISPATCH_TILE, t)
    zeros = jnp.zeros((n_rows, d), h2.dtype)
    return pl.pallas_call(
        _dispatch_kernel,
        grid=(t // tm,),
        in_specs=[pl.BlockSpec((2, tm), lambda i: (0, i), memory_space=pltpu.SMEM),
                  pl.BlockSpec((tm, d), lambda i: (i, 0)),
                  pl.BlockSpec(memory_space=pl.ANY)],
        out_specs=pl.BlockSpec(memory_space=pl.ANY),
        out_shape=jax.ShapeDtypeStruct((n_rows, d), h2.dtype),
        scratch_shapes=[pltpu.SemaphoreType.DMA],
        input_output_aliases={2: 0},
        compiler_params=pltpu.CompilerParams(dimension_semantics=("arbitrary",)),
        name="dispatch",
    )(pos, h2, zeros)


def _expert_kernel(te_ref, xs_ref, wgu_ref, wd_ref, ys_ref):
    del te_ref
    gu = _dot(xs_ref[...].astype(BF16), wgu_ref[0])
    gate = gu[:, :D_EXPERT]
    hid = gate * (1.0 / (1.0 + jnp.exp(-gate))) * gu[:, D_EXPERT:]
    ys_ref[...] = _dot(hid.astype(BF16), wd_ref[0])


def _experts(te, xs, wgu, wd):
    n_rows, d = xs.shape
    tm = ROW_TILE
    return pl.pallas_call(
        _expert_kernel,
        grid_spec=pltpu.PrefetchScalarGridSpec(
            num_scalar_prefetch=1,
            grid=(n_rows // tm,),
            in_specs=[pl.BlockSpec((tm, d), lambda t, te: (t, 0)),
                      pl.BlockSpec((1, d, 2 * D_EXPERT), lambda t, te: (te[t], 0, 0)),
                      pl.BlockSpec((1, D_EXPERT, d), lambda t, te: (te[t], 0, 0))],
            out_specs=pl.BlockSpec((tm, d), lambda t, te: (t, 0))),
        out_shape=jax.ShapeDtypeStruct((n_rows, d), F32),
        compiler_params=pltpu.CompilerParams(
            dimension_semantics=("arbitrary",), vmem_limit_bytes=VMEM_LIMIT),
        name="expert",
    )(te, xs, wgu, wd)


def _combine_kernel(pos_ref, posn_ref, x1_ref, wcol_ref, g2_ref, lg_ref, lb_ref, ys_hbm,
                    o_ref, gbuf, sem):
    tm = pos_ref.shape[1]
    i = pl.program_id(0)
    n = pl.num_programs(0)
    cur = lax.rem(i, 2)

    def row_copy(p_ref, slot, r, buf):
        return pltpu.make_async_copy(ys_hbm.at[pl.ds(p_ref[slot, r], 1), :],
                                     gbuf.at[buf, slot, pl.ds(r, 1), :], sem.at[buf])

    def issue(p_ref, buf):
        def body(r, _):
            row_copy(p_ref, 0, r, buf).start()
            row_copy(p_ref, 1, r, buf).start()
            return 0
        lax.fori_loop(0, tm, body, 0, unroll=8)

    @pl.when(i == 0)
    def _():
        issue(pos_ref, 0)

    @pl.when(i + 1 < n)
    def _():
        issue(posn_ref, 1 - cur)

    def drain(r, _):
        row_copy(pos_ref, 0, r, cur).wait()
        row_copy(pos_ref, 1, r, cur).wait()
        return 0
    lax.fori_loop(0, tm, drain, 0, unroll=8)

    wc = wcol_ref[...]
    moe = wc[:, 0:1] * gbuf[cur, 0] + wc[:, 1:2] * gbuf[cur, 1]
    o_ref[...] = _layer_norm(DEEPNORM_ALPHA * x1_ref[...] + g2_ref[0] * moe, lg_ref[...], lb_ref[...])


def _combine(pos, x1, wcol, gate2, ln_g, ln_b, ys, seq):
    t, d = x1.shape
    tm = min(ROW_TILE, seq)
    n = t // tm
    per_seq = seq // tm
    full = lambda a: pl.BlockSpec(a.shape, lambda i: (0,) * a.ndim)
    return pl.pallas_call(
        _combine_kernel,
        grid=(n,),
        in_specs=[pl.BlockSpec((2, tm), lambda i: (0, i), memory_space=pltpu.SMEM),
                  pl.BlockSpec((2, tm), lambda i: (0, jnp.minimum(i + 1, n - 1)), memory_space=pltpu.SMEM),
                  pl.BlockSpec((tm, d), lambda i: (i, 0)),
                  pl.BlockSpec((tm, LANES), lambda i: (i, 0)),
                  pl.BlockSpec((1, 1, d), lambda i: (i // per_seq, 0, 0)),
                  full(ln_g), full(ln_b),
                  pl.BlockSpec(memory_space=pl.ANY)],
        out_specs=pl.BlockSpec((tm, d), lambda i: (i, 0)),
        out_shape=jax.ShapeDtypeStruct((t, d), F32),
        scratch_shapes=[pltpu.VMEM((2, 2, tm, d), F32), pltpu.SemaphoreType.DMA((2,))],
        compiler_params=pltpu.CompilerParams(
            dimension_semantics=("arbitrary",), vmem_limit_bytes=VMEM_LIMIT),
        name="combine",
    )(pos, pos, x1, wcol, gate2, ln_g, ln_b, ys)


def kernel(x, c, w_ada, b_ada, w_in, conv_w, w_out, ln1_g, ln1_b, w_router_group, b_router_group,
           w_router_expert, b_router_expert, w_gate, w_up, w_down, ln2_g, ln2_b):
    b, s, d = x.shape
    t = b * s
    assert d == D_ATTN + D_CONV and w_ada.shape[0] == DEPTH
    assert s % min(QUERY_BLOCK, s) == 0 and s % min(SEQ_TILE, s) == 0 and s % KEY_BLOCK == 0
    assert (2 * t) % POS_CHUNK == 0 and b <= SUBLANES

    c8 = jnp.zeros((SUBLANES, d), F32).at[:b].set(c)
    ada = _ada(c8, w_ada[0], b_ada[0][None, :])[:b].reshape(b, N_ADA, 1, d)
    shift1, scale1, gate1, shift2, scale2, gate2 = [ada[:, i] for i in range(N_ADA)]

    wi = w_in[0]
    q_scale = LOG2E * HEAD_DIM ** -0.5
    wqkv = jnp.concatenate([wi[:, :D_ATTN] * q_scale, wi[:, D_ATTN:3 * D_ATTN]], axis=1).astype(BF16)
    wc = wi[:, 3 * D_ATTN:].astype(BF16)
    wo = w_out[0].astype(BF16)
    wr = jnp.concatenate([w_router_expert[0].T, w_router_group[0].T,
                          jnp.zeros((ROUTER_ROWS - N_EXPERTS - N_GROUPS, d), F32)], axis=0)
    wr_hi = wr.astype(BF16)
    wr_lo = (wr - wr_hi.astype(F32)).astype(BF16)
    rb = jnp.concatenate([b_router_expert[0], b_router_group[0],
                          jnp.full((ROUTER_ROWS - N_EXPERTS - N_GROUPS,), -1e30, F32)])[:, None]
    wgu = jnp.concatenate([w_gate[0], w_up[0]], axis=-1).astype(BF16)
    wd = w_down[0].astype(BF16)
    tri = jnp.tril(jnp.ones((KEY_BLOCK, KEY_BLOCK), BF16), -1)

    q, k, v, conv = _inproj(x, scale1, shift1, wqkv, wc, conv_w[0])
    attn = _attn(q, k, v, tri)
    x1, h2, eid, wcol = _outproj(attn, conv, x, gate1, scale2, shift2, ln1_g, ln1_b,
                                 wo, wr_hi, wr_lo, rb)

    n_tiles = (2 * t) // ROW_TILE + N_EXPERTS
    pos, te = _positions(eid, n_tiles)
    xs = _dispatch(pos, h2.reshape(t, d), n_tiles * ROW_TILE)
    ys = _experts(te, xs, wgu, wd)
    out = _combine(pos, x1.reshape(t, d), wcol, gate2, ln2_g, ln2_b, ys, s)
    return out.reshape(b, s, d)
```

```python
import functools
import math

import jax
import jax.numpy as jnp
from jax import lax
from jax.experimental import pallas as pl
from jax.experimental.pallas import tpu as pltpu

F32 = jnp.float32
BF16 = jnp.bfloat16
I32 = jnp.int32

D_ATTN = 512
HEAD_DIM = 64
D_CONV = 512
CONV_WIDTH = 3
N_GROUPS = 4
EXPERTS_PER_GROUP = 8
N_EXPERTS = N_GROUPS * EXPERTS_PER_GROUP
D_EXPERT = 256
N_ADA = 6
DEPTH = 1
DEEPNORM_ALPHA = (2.0 * DEPTH) ** 0.25
LN_EPS = 1e-5
LOG2E = math.log2(math.e)

LANES = 128
SUBLANES = 8

SEQ_TILE = 512
KEY_BLOCK = 256
QUERY_BLOCK = 512
PAIRS_PER_STEP = 2
ROW_TILE = 256
DISPATCH_TILE = 512
POS_CHUNK = 256
ROUTER_ROWS = 40
VMEM_LIMIT = 48 * 1024 * 1024

_NT = (((1,), (1,)), ((), ()))


def _dot(a, b):
    return jnp.dot(a, b, preferred_element_type=F32)


def _dot_nt(a, b):
    return lax.dot_general(a, b, _NT, preferred_element_type=F32)


def _split_bf16(x):
    hi = x.astype(BF16)
    lo = (x - hi.astype(F32)).astype(BF16)
    return hi, lo


def _layer_norm(y, g, b):
    mu = jnp.mean(y, axis=-1, keepdims=True)
    yc = y - mu
    var = jnp.mean(yc * yc, axis=-1, keepdims=True)
    return yc * lax.rsqrt(var + LN_EPS) * g + b


def _ada_kernel(c_ref, w_ref, b_ref, o_ref):
    c = c_ref[...]
    s = c * (1.0 / (1.0 + jnp.exp(-c)))
    s_hi, s_lo = _split_bf16(s)
    w_hi, w_lo = _split_bf16(w_ref[...])
    o_ref[...] = _dot(s_hi, w_hi) + _dot(s_hi, w_lo) + _dot(s_lo, w_hi) + b_ref[...]


def _ada(c8, w, b):
    d = c8.shape[1]
    n = w.shape[1]
    return pl.pallas_call(
        _ada_kernel,
        grid=(n // d,),
        in_specs=[pl.BlockSpec((SUBLANES, d), lambda j: (0, 0)),
                  pl.BlockSpec((d, d), lambda j: (0, j)),
                  pl.BlockSpec((1, d), lambda j: (0, j))],
        out_specs=pl.BlockSpec((SUBLANES, d), lambda j: (0, j)),
        out_shape=jax.ShapeDtypeStruct((SUBLANES, n), F32),
        name="ada",
    )(c8, w, b)


def _inproj_kernel(x_ref, sc_ref, sh_ref, wqkv_ref, wc_ref, cw_ref,
                   q_ref, k_ref, v_ref, conv_ref, carry_ref):
    ts = x_ref.shape[1]

    @pl.when(pl.program_id(1) == 0)
    def _():
        carry_ref[...] = jnp.zeros_like(carry_ref)

    h = (x_ref[0] * (1.0 + sc_ref[0]) + sh_ref[0]).astype(BF16)
    qkv = _dot(h, wqkv_ref[...])
    q_ref[0] = qkv[:, :D_ATTN].astype(BF16)
    k_ref[0] = qkv[:, D_ATTN:2 * D_ATTN].astype(BF16)
    v_ref[0] = qkv[:, 2 * D_ATTN:].astype(BF16)

    c3 = _dot(h, wc_ref[...])
    cb = c3[:, :D_CONV]
    u = c3[:, D_CONV:2 * D_CONV] * c3[:, 2 * D_CONV:]
    prev = carry_ref[...]
    row = lax.broadcasted_iota(I32, u.shape, 0)
    u1 = jnp.where(row == 0, prev[SUBLANES - 1:SUBLANES, :], pltpu.roll(u, 1, 0))
    u2 = jnp.where(row == 0, prev[SUBLANES - 2:SUBLANES - 1, :],
                   jnp.where(row == 1, prev[SUBLANES - 1:SUBLANES, :], pltpu.roll(u, 2, 0)))
    cw = cw_ref[...]
    conv = cb * (cw[0:1, :] * u2 + cw[1:2, :] * u1 + cw[2:3, :] * u)
    conv_ref[0] = conv.astype(BF16)
    carry_ref[...] = u[ts - SUBLANES:, :]


def _inproj(x, scale1, shift1, wqkv, wc, conv_w):
    b, s, d = x.shape
    ts = min(SEQ_TILE, s)
    vec = pl.BlockSpec((1, 1, d), lambda i, j: (i, 0, 0))
    full = lambda a: pl.BlockSpec(a.shape, lambda i, j: (0,) * a.ndim)
    tile = lambda w: pl.BlockSpec((1, ts, w), lambda i, j: (i, j, 0))
    act = lambda w: jax.ShapeDtypeStruct((b, s, w), BF16)
    return pl.pallas_call(
        _inproj_kernel,
        grid=(b, s // ts),
        in_specs=[tile(d), vec, vec, full(wqkv), full(wc), full(conv_w)],
        out_specs=[tile(D_ATTN), tile(D_ATTN), tile(D_ATTN), tile(D_CONV)],
        out_shape=[act(D_ATTN), act(D_ATTN), act(D_ATTN), act(D_CONV)],
        scratch_shapes=[pltpu.VMEM((SUBLANES, D_CONV), F32)],
        compiler_params=pltpu.CompilerParams(
            dimension_semantics=("arbitrary", "arbitrary"), vmem_limit_bytes=VMEM_LIMIT),
        name="inproj",
    )(x, scale1, shift1, wqkv, wc, conv_w)


def _attn_tile(qs, kj, vj, tri, carry, acc, causal):
    z = _dot_nt(qs, kj)
    sp = jnp.maximum(z, 0.0) + jnp.log(1.0 + jnp.exp2(-jnp.abs(z))) * LOG2E
    if causal is not None:
        sp = jnp.where(causal, sp, 0.0)
    sp16 = sp.astype(BF16)
    after = _dot(sp16, tri)
    a = jnp.exp2(z - (sp + (after + carry)))
    if causal is not None:
        a = jnp.where(causal, a, 0.0)
    acc = acc + _dot(a.astype(BF16), vj)
    carry = carry + (after[:, 0:1] + sp16[:, 0:1].astype(F32))
    return carry, acc


def _attn_kernel(q_ref, k_ref, v_ref, tri_ref, o_ref):
    s = q_ref.shape[1]
    bk = tri_ref.shape[0]
    bq = min(QUERY_BLOCK, s)
    per_q = bq // bk
    rows = 2 * bq
    pw = 2 * HEAD_DIM
    npairs = q_ref.shape[2] // pw
    lane = lax.broadcasted_iota(I32, (bq, pw), 1)
    qrow = lax.broadcasted_iota(I32, (rows, bk), 0) & (bq - 1)
    kcol = lax.broadcasted_iota(I32, (rows, bk), 1)

    def q_block(qi, _):
        q0 = pl.multiple_of(qi * bq, bq)
        qs = []
        for p in range(npairs):
            q2 = q_ref[0, pl.ds(q0, bq), p * pw:(p + 1) * pw]
            zero = jnp.zeros_like(q2)
            qs.append(jnp.concatenate([jnp.where(lane < HEAD_DIM, q2, zero),
                                       jnp.where(lane >= HEAD_DIM, q2, zero)], axis=0))

        def tiles(j, state, causal):
            k0 = pl.multiple_of(j * bk, bk)
            out = []
            for p in range(npairs):
                out.extend(_attn_tile(qs[p], k_ref[0, pl.ds(k0, bk), p * pw:(p + 1) * pw],
                                      v_ref[0, pl.ds(k0, bk), p * pw:(p + 1) * pw],
                                      tri_ref[...], state[2 * p], state[2 * p + 1], causal))
            return tuple(out)

        state = (jnp.zeros((rows, 1), F32), jnp.zeros((rows, pw), F32)) * npairs
        for d in reversed(range(per_q)):
            state = tiles(qi * per_q + d, state, kcol + d * bk < qrow)
        def visible(t, st):
            for d in range(per_q):
                st = tiles((qi - t) * per_q - 1 - d, st, None)
            return st
        state = lax.fori_loop(0, qi, visible, state)
        for p in range(npairs):
            acc = state[2 * p + 1]
            o_ref[0, pl.ds(q0, bq), p * pw:(p + 1) * pw] = (
                jnp.where(lane < HEAD_DIM, acc[:bq], acc[bq:]).astype(BF16))
        return 0

    lax.fori_loop(0, s // bq, q_block, 0)


def _attn(q, k, v, tri):
    b, s, _ = q.shape
    width = PAIRS_PER_STEP * 2 * HEAD_DIM
    spec = pl.BlockSpec((1, s, width), lambda i, j: (i, 0, j))
    return pl.pallas_call(
        _attn_kernel,
        grid=(b, D_ATTN // width),
        in_specs=[spec, spec, spec, pl.BlockSpec(tri.shape, lambda i, j: (0, 0))],
        out_specs=spec,
        out_shape=jax.ShapeDtypeStruct((b, s, D_ATTN), BF16),
        compiler_params=pltpu.CompilerParams(
            dimension_semantics=("arbitrary", "arbitrary"), vmem_limit_bytes=VMEM_LIMIT),
        name="attn",
    )(q, k, v, tri)


def _outproj_kernel(attn_ref, conv_ref, x_ref, g1_ref, sc_ref, sh_ref, lg_ref, lb_ref,
                    wo_ref, wrh_ref, wrl_ref, rb_ref,
                    x1_ref, eid_ref, wcol_ref):
    ts = x_ref.shape[1]
    cat = jnp.concatenate([attn_ref[0], conv_ref[0]], axis=-1)
    mix = _dot(cat, wo_ref[...])
    x1 = _layer_norm(DEEPNORM_ALPHA * x_ref[0] + g1_ref[0] * mix, lg_ref[...], lb_ref[...])
    x1_ref[0] = x1
    h2 = x1 * (1.0 + sc_ref[0]) + sh_ref[0]

    h_hi, h_lo = _split_bf16(h2)
    wrh = wrh_ref[...]
    logit = _dot_nt(wrh, h_hi) + _dot_nt(wrl_ref[...], h_hi) + _dot_nt(wrh, h_lo) + rb_ref[...]

    e = EXPERTS_PER_GROUP
    rio = lax.broadcasted_iota(I32, (e, ts), 0)
    first_where = lambda hit: jnp.min(jnp.where(hit, rio, e), axis=0, keepdims=True)

    gl = logit[N_EXPERTS:N_EXPERTS + e]
    gmax = jnp.max(gl, axis=0, keepdims=True)
    g_prob = 1.0 / jnp.sum(jnp.exp(gl - gmax), axis=0, keepdims=True)
    g_idx = first_where(gl == gmax)
    sel = logit[0:e]
    for g in range(1, N_GROUPS):
        sel = jnp.where(g_idx == g, logit[g * e:(g + 1) * e], sel)
    m1 = jnp.max(sel, axis=0, keepdims=True)
    i1 = first_where(sel == m1)
    sel2 = jnp.where(rio == i1, -jnp.inf, sel)
    m2 = jnp.max(sel2, axis=0, keepdims=True)
    i2 = first_where(sel2 == m2)
    r = jnp.exp(m2 - m1)
    w1 = g_prob / (1.0 + r)
    w2 = w1 * r
    eid_ref[0:1, :] = g_idx * e + i1
    eid_ref[1:2, :] = g_idx * e + i2
    wrows = jnp.concatenate([w1, w2, jnp.zeros((LANES - 2, ts), F32)], axis=0)
    wcol_ref[...] = wrows.T


def _outproj(attn, conv, x, gate1, scale2, shift2, ln_g, ln_b, wo, wr_hi, wr_lo, rb):
    b, s, d = x.shape
    ts = min(SEQ_TILE, s)
    nj = s // ts
    vec = pl.BlockSpec((1, 1, d), lambda i, j: (i, 0, 0))
    full = lambda a: pl.BlockSpec(a.shape, lambda i, j: (0,) * a.ndim)
    tile = lambda w: pl.BlockSpec((1, ts, w), lambda i, j: (i, j, 0))
    return pl.pallas_call(
        _outproj_kernel,
        grid=(b, nj),
        in_specs=[tile(D_ATTN), tile(D_CONV), tile(d), vec, vec, vec, full(ln_g), full(ln_b),
                  full(wo), full(wr_hi), full(wr_lo), full(rb)],
        out_specs=[tile(d),
                   pl.BlockSpec((2, ts), lambda i, j: (0, i * nj + j)),
                   pl.BlockSpec((ts, LANES), lambda i, j: (i * nj + j, 0))],
        out_shape=[jax.ShapeDtypeStruct((b, s, d), F32),
                   jax.ShapeDtypeStruct((2, b * s), I32),
                   jax.ShapeDtypeStruct((b * s, LANES), F32)],
        compiler_params=pltpu.CompilerParams(
            dimension_semantics=("arbitrary", "arbitrary"), vmem_limit_bytes=VMEM_LIMIT),
        name="outproj",
    )(attn, conv, x, gate1, scale2, shift2, ln_g, ln_b, wo, wr_hi, wr_lo, rb)


def _pos_kernel(eid_ref, pos_ref, te_ref, *, row_tile):
    nchunk, ch = eid_ref.shape
    eio = lax.broadcasted_iota(I32, (N_EXPERTS, ch), 0)
    ones = jnp.ones((ch, LANES), BF16)
    tile_lanes = te_ref.shape[1]

    def onehot(c):
        return eid_ref[pl.ds(c, 1), :] == eio

    cnt = lax.fori_loop(0, nchunk, lambda c, acc: acc + _dot(onehot(c).astype(BF16), ones),
                        jnp.zeros((N_EXPERTS, LANES), F32))
    ntile = jnp.floor((cnt + (row_tile - 1)) * (1.0 / row_tile))
    lower = (lax.broadcasted_iota(I32, (N_EXPERTS, N_EXPERTS), 1)
             < lax.broadcasted_iota(I32, (N_EXPERTS, N_EXPERTS), 0)).astype(BF16)
    first_tile = _dot(lower, ntile.astype(BF16))
    end_tile = first_tile + ntile

    tio = lax.broadcasted_iota(I32, (N_EXPERTS, tile_lanes), 1).astype(F32)
    ends = jnp.concatenate([end_tile] * (tile_lanes // LANES), axis=1)
    owner = jnp.sum((ends <= tio).astype(F32), axis=0, keepdims=True)
    te_ref[...] = jnp.minimum(owner, N_EXPERTS - 1).astype(I32)

    before = (lax.broadcasted_iota(I32, (ch, ch), 0)
              < lax.broadcasted_iota(I32, (ch, ch), 1)).astype(BF16)

    def place(c, run):
        hit = onehot(c)
        hit16 = hit.astype(BF16)
        rank = _dot(hit16, before)
        base = jnp.concatenate([run] * (ch // LANES), axis=1)
        p = jnp.sum(jnp.where(hit, rank + base, 0.0), axis=0, keepdims=True)
        pos_ref[pl.ds(c, 1), :] = p.astype(I32)
        return run + _dot(hit16, ones)

    lax.fori_loop(0, nchunk, place, first_tile * row_tile)


def _positions(eid, n_tiles):
    npairs = eid.shape[0] * eid.shape[1]
    nchunk = npairs // POS_CHUNK
    tile_lanes = pl.cdiv(n_tiles, LANES) * LANES
    pos, te = pl.pallas_call(
        functools.partial(_pos_kernel, row_tile=ROW_TILE),
        out_shape=[jax.ShapeDtypeStruct((nchunk, POS_CHUNK), I32),
                   jax.ShapeDtypeStruct((1, tile_lanes), I32)],
        name="pos",
    )(eid.reshape(nchunk, POS_CHUNK))
    return pos.reshape(npairs), te.reshape(tile_lanes)


def _load_tokens(ref, base, n):
    return jnp.concatenate([ref[pl.ds(base + a, n, stride=SUBLANES), :] for a in range(SUBLANES)], axis=1)


def _store_tokens(ref, base, val):
    n = val.shape[0]
    for a in range(SUBLANES):
        ref[pl.ds(base + a, n, stride=SUBLANES), :] = val[:, a * LANES:(a + 1) * LANES]


def _dispatch_kernel(pos0_ref, pos1_ref, x1_ref, sc_ref, sh_ref, zero_hbm, dst_hbm, stage, sem):
    del zero_hbm
    pos_refs = (pos0_ref, pos1_ref)
    tm = pos0_ref.shape[0]
    i = pl.program_id(0)
    n = pl.num_programs(0)
    cur = lax.rem(i, 2)

    def token_copy(slot, r, buf):
        dst = pl.multiple_of(pos_refs[slot][r] * SUBLANES, SUBLANES)
        src = pl.multiple_of((buf * tm + r) * SUBLANES, SUBLANES)
        return pltpu.make_async_copy(stage.at[pl.ds(src, SUBLANES), :],
                                     dst_hbm.at[pl.ds(dst, SUBLANES), :], sem.at[buf])

    def drain(buf):
        def body(r, _):
            token_copy(0, r, buf).wait()
            token_copy(1, r, buf).wait()
            return 0
        lax.fori_loop(0, tm, body, 0, unroll=8)

    h2 = x1_ref[...] * (1.0 + sc_ref[0]) + sh_ref[0]
    _store_tokens(stage, cur * (tm * SUBLANES), h2)

    def issue(r, _):
        token_copy(0, r, cur).start(priority=0)
        token_copy(1, r, cur).start(priority=1)
        return 0
    lax.fori_loop(0, tm, issue, 0, unroll=8)

    @pl.when(i > 0)
    def _():
        drain(1 - cur)

    @pl.when(i == n - 1)
    def _():
        drain(cur)


def _dispatch(pos, x1, scale2, shift2, n_rows, seq):
    t, d = x1.shape
    assert d == SUBLANES * LANES
    tm = min(DISPATCH_TILE, seq)
    per_seq = seq // tm
    vec = pl.BlockSpec((1, 1, d), lambda i: (i // per_seq, 0, 0))
    zeros = jnp.zeros((n_rows * SUBLANES, LANES), F32)
    nb = t // tm
    return pl.pallas_call(
        _dispatch_kernel,
        grid=(nb,),
        in_specs=[pl.BlockSpec((tm,), lambda i: (i,), memory_space=pltpu.SMEM),
                  pl.BlockSpec((tm,), lambda i: (nb + i,), memory_space=pltpu.SMEM),
                  pl.BlockSpec((tm, d), lambda i: (i, 0)), vec, vec,
                  pl.BlockSpec(memory_space=pl.ANY)],
        out_specs=pl.BlockSpec(memory_space=pl.ANY),
        out_shape=jax.ShapeDtypeStruct((n_rows * SUBLANES, LANES), F32),
        scratch_shapes=[pltpu.VMEM((2 * tm * SUBLANES, LANES), F32), pltpu.SemaphoreType.DMA((2,))],
        input_output_aliases={5: 0},
        compiler_params=pltpu.CompilerParams(
            dimension_semantics=("arbitrary",), vmem_limit_bytes=VMEM_LIMIT),
        name="dispatch",
    )(pos, pos, x1, scale2, shift2, zeros)


def _expert_kernel(te_ref, xs_ref, wgu_ref, wd_ref, ys_ref):
    del te_ref
    tm = xs_ref.shape[0] // SUBLANES
    gu = _dot(_load_tokens(xs_ref, 0, tm).astype(BF16), wgu_ref[0])
    gate = gu[:, :D_EXPERT]
    hid = gate * (1.0 / (1.0 + jnp.exp(-gate))) * gu[:, D_EXPERT:]
    _store_tokens(ys_ref, 0, _dot(hid.astype(BF16), wd_ref[0]))


def _experts(te, xs, wgu, wd):
    n_rows = xs.shape[0] // SUBLANES
    tm = ROW_TILE
    d = SUBLANES * LANES
    tok = pl.BlockSpec((tm * SUBLANES, LANES), lambda t, te: (t, 0))
    return pl.pallas_call(
        _expert_kernel,
        grid_spec=pltpu.PrefetchScalarGridSpec(
            num_scalar_prefetch=1,
            grid=(n_rows // tm,),
            in_specs=[tok,
                      pl.BlockSpec((1, d, 2 * D_EXPERT), lambda t, te: (te[t], 0, 0)),
                      pl.BlockSpec((1, D_EXPERT, d), lambda t, te: (te[t], 0, 0))],
            out_specs=tok),
        out_shape=jax.ShapeDtypeStruct(xs.shape, F32),
        compiler_params=pltpu.CompilerParams(
            dimension_semantics=("arbitrary",), vmem_limit_bytes=VMEM_LIMIT),
        name="expert",
    )(te, xs, wgu, wd)


def _combine_kernel(pos0_ref, pos1_ref, nxt0_ref, nxt1_ref, x1_ref, wcol_ref, g2_ref, lg_ref, lb_ref,
                    ys_hbm, o_ref, gbuf, sem):
    pos_ref = (pos0_ref, pos1_ref)
    posn_ref = (nxt0_ref, nxt1_ref)
    tm = pos0_ref.shape[0]
    i = pl.program_id(0)
    n = pl.num_programs(0)
    cur = lax.rem(i, 2)
    base = lambda buf, slot: (buf * 2 + slot) * (tm * SUBLANES)

    def token_copy(p_ref, slot, r, buf):
        src = pl.multiple_of(p_ref[slot][r] * SUBLANES, SUBLANES)
        dst = pl.multiple_of(base(buf, slot) + r * SUBLANES, SUBLANES)
        return pltpu.make_async_copy(ys_hbm.at[pl.ds(src, SUBLANES), :],
                                     gbuf.at[pl.ds(dst, SUBLANES), :], sem.at[buf])

    def issue(p_ref, buf):
        def body(r, _):
            token_copy(p_ref, 0, r, buf).start(priority=0)
            token_copy(p_ref, 1, r, buf).start(priority=1)
            return 0
        lax.fori_loop(0, tm, body, 0, unroll=8)

    @pl.when(i == 0)
    def _():
        issue(pos_ref, 0)

    @pl.when(i + 1 < n)
    def _():
        issue(posn_ref, 1 - cur)

    def drain(r, _):
        token_copy(pos_ref, 0, r, cur).wait()
        token_copy(pos_ref, 1, r, cur).wait()
        return 0
    lax.fori_loop(0, tm, drain, 0, unroll=8)

    wc = wcol_ref[...]
    moe = (wc[:, 0:1] * _load_tokens(gbuf, base(cur, 0), tm)
           + wc[:, 1:2] * _load_tokens(gbuf, base(cur, 1), tm))
    o_ref[...] = _layer_norm(DEEPNORM_ALPHA * x1_ref[...] + g2_ref[0] * moe, lg_ref[...], lb_ref[...])


def _combine(pos, x1, wcol, gate2, ln_g, ln_b, ys, seq):
    t, d = x1.shape
    tm = min(ROW_TILE, seq)
    n = t // tm
    per_seq = seq // tm
    full = lambda a: pl.BlockSpec(a.shape, lambda i: (0,) * a.ndim)
    smem = lambda index_map: pl.BlockSpec((tm,), index_map, memory_space=pltpu.SMEM)
    return pl.pallas_call(
        _combine_kernel,
        grid=(n,),
        in_specs=[smem(lambda i: (i,)), smem(lambda i: (n + i,)),
                  smem(lambda i: (jnp.minimum(i + 1, n - 1),)),
                  smem(lambda i: (n + jnp.minimum(i + 1, n - 1),)),
                  pl.BlockSpec((tm, d), lambda i: (i, 0)),
                  pl.BlockSpec((tm, LANES), lambda i: (i, 0)),
                  pl.BlockSpec((1, 1, d), lambda i: (i // per_seq, 0, 0)),
                  full(ln_g), full(ln_b),
                  pl.BlockSpec(memory_space=pl.ANY)],
        out_specs=pl.BlockSpec((tm, d), lambda i: (i, 0)),
        out_shape=jax.ShapeDtypeStruct((t, d), F32),
        scratch_shapes=[pltpu.VMEM((4 * tm * SUBLANES, LANES), F32), pltpu.SemaphoreType.DMA((2,))],
        compiler_params=pltpu.CompilerParams(
            dimension_semantics=("arbitrary",), vmem_limit_bytes=VMEM_LIMIT),
        name="combine",
    )(pos, pos, pos, pos, x1, wcol, gate2, ln_g, ln_b, ys)


def kernel(x, c, w_ada, b_ada, w_in, conv_w, w_out, ln1_g, ln1_b, w_router_group, b_router_group,
           w_router_expert, b_router_expert, w_gate, w_up, w_down, ln2_g, ln2_b):
    b, s, d = x.shape
    t = b * s
    assert d == D_ATTN + D_CONV and w_ada.shape[0] == DEPTH
    assert s % min(QUERY_BLOCK, s) == 0 and s % min(SEQ_TILE, s) == 0 and s % KEY_BLOCK == 0
    assert (2 * t) % POS_CHUNK == 0 and b <= SUBLANES

    c8 = jnp.zeros((SUBLANES, d), F32).at[:b].set(c)
    ada = _ada(c8, w_ada[0], b_ada[0][None, :])[:b].reshape(b, N_ADA, 1, d)
    shift1, scale1, gate1, shift2, scale2, gate2 = [ada[:, i] for i in range(N_ADA)]

    wi = w_in[0]
    q_scale = LOG2E * HEAD_DIM ** -0.5
    wqkv = jnp.concatenate([wi[:, :D_ATTN] * q_scale, wi[:, D_ATTN:3 * D_ATTN]], axis=1).astype(BF16)
    wc = wi[:, 3 * D_ATTN:].astype(BF16)
    wo = w_out[0].astype(BF16)
    wr = jnp.concatenate([w_router_expert[0].T, w_router_group[0].T,
                          jnp.zeros((ROUTER_ROWS - N_EXPERTS - N_GROUPS, d), F32)], axis=0)
    wr_hi = wr.astype(BF16)
    wr_lo = (wr - wr_hi.astype(F32)).astype(BF16)
    rb = jnp.concatenate([b_router_expert[0], b_router_group[0],
                          jnp.full((ROUTER_ROWS - N_EXPERTS - N_GROUPS,), -1e30, F32)])[:, None]
    wgu = jnp.concatenate([w_gate[0], w_up[0]], axis=-1).astype(BF16)
    wd = w_down[0].astype(BF16)
    tri = jnp.tril(jnp.ones((KEY_BLOCK, KEY_BLOCK), BF16), -1)

    q, k, v, conv = _inproj(x, scale1, shift1, wqkv, wc, conv_w[0])
    attn = _attn(q, k, v, tri)
    x1, eid, wcol = _outproj(attn, conv, x, gate1, scale2, shift2, ln1_g, ln1_b,
                             wo, wr_hi, wr_lo, rb)
    x1 = x1.reshape(t, d)

    n_tiles = (2 * t) // ROW_TILE + N_EXPERTS
    pos, te = _positions(eid, n_tiles)
    xs = _dispatch(pos, x1, scale2, shift2, n_tiles * ROW_TILE, s)
    ys = _experts(te, xs, wgu, wd)
    out = _combine(pos, x1, wcol, gate2, ln2_g, ln2_b, ys, s)
    return out.reshape(b, s, d)
```

```python
import functools
import math

import jax
import jax.numpy as jnp
from jax import lax
from jax.experimental import pallas as pl
from jax.experimental.pallas import tpu as pltpu

F32 = jnp.float32
BF16 = jnp.bfloat16
I32 = jnp.int32

D_ATTN = 512
HEAD_DIM = 64
D_CONV = 512
CONV_WIDTH = 3
N_GROUPS = 4
EXPERTS_PER_GROUP = 8
N_EXPERTS = N_GROUPS * EXPERTS_PER_GROUP
D_EXPERT = 256
N_ADA = 6
DEPTH = 1
DEEPNORM_ALPHA = (2.0 * DEPTH) ** 0.25
LN_EPS = 1e-5
LOG2E = math.log2(math.e)

LANES = 128
SUBLANES = 8

SEQ_TILE = 512
KEY_BLOCK = 256
QUERY_BLOCK = 512
PAIRS_PER_STEP = 2
ROW_TILE = 256
DISPATCH_TILE = 512
POS_CHUNK = 256
ROUTER_ROWS = 40
VMEM_LIMIT = 48 * 1024 * 1024

_NT = (((1,), (1,)), ((), ()))


def _dot(a, b):
    return jnp.dot(a, b, preferred_element_type=F32)


def _dot_nt(a, b):
    return lax.dot_general(a, b, _NT, preferred_element_type=F32)


def _split_bf16(x):
    hi = x.astype(BF16)
    lo = (x - hi.astype(F32)).astype(BF16)
    return hi, lo


def _layer_norm(y, g, b):
    mu = jnp.mean(y, axis=-1, keepdims=True)
    yc = y - mu
    var = jnp.mean(yc * yc, axis=-1, keepdims=True)
    return yc * lax.rsqrt(var + LN_EPS) * g + b


def _ada_kernel(c_ref, w_ref, b_ref, o_ref):
    c = c_ref[...]
    s = c * (1.0 / (1.0 + jnp.exp(-c)))
    s_hi, s_lo = _split_bf16(s)
    w_hi, w_lo = _split_bf16(w_ref[...])
    o_ref[...] = _dot(s_hi, w_hi) + _dot(s_hi, w_lo) + _dot(s_lo, w_hi) + b_ref[...]


def _ada(c8, w, b):
    d = c8.shape[1]
    n = w.shape[1]
    return pl.pallas_call(
        _ada_kernel,
        grid=(n // d,),
        in_specs=[pl.BlockSpec((SUBLANES, d), lambda j: (0, 0)),
                  pl.BlockSpec((d, d), lambda j: (0, j)),
                  pl.BlockSpec((1, d), lambda j: (0, j))],
        out_specs=pl.BlockSpec((SUBLANES, d), lambda j: (0, j)),
        out_shape=jax.ShapeDtypeStruct((SUBLANES, n), F32),
        name="ada",
    )(c8, w, b)


def _inproj_kernel(x_ref, sc_ref, sh_ref, wqkv_ref, wc_ref, cw_ref,
                   q_ref, k_ref, v_ref, conv_ref, carry_ref):
    ts = x_ref.shape[1]

    @pl.when(pl.program_id(1) == 0)
    def _():
        carry_ref[...] = jnp.zeros_like(carry_ref)

    h = (x_ref[0] * (1.0 + sc_ref[0]) + sh_ref[0]).astype(BF16)
    qkv = _dot(h, wqkv_ref[...])
    q_ref[0] = qkv[:, :D_ATTN].astype(BF16)
    k_ref[0] = qkv[:, D_ATTN:2 * D_ATTN].astype(BF16)
    v_ref[0] = qkv[:, 2 * D_ATTN:].astype(BF16)

    c3 = _dot(h, wc_ref[...])
    cb = c3[:, :D_CONV]
    u = c3[:, D_CONV:2 * D_CONV] * c3[:, 2 * D_CONV:]
    prev = carry_ref[...]
    row = lax.broadcasted_iota(I32, u.shape, 0)
    u1 = jnp.where(row == 0, prev[SUBLANES - 1:SUBLANES, :], pltpu.roll(u, 1, 0))
    u2 = jnp.where(row == 0, prev[SUBLANES - 2:SUBLANES - 1, :],
                   jnp.where(row == 1, prev[SUBLANES - 1:SUBLANES, :], pltpu.roll(u, 2, 0)))
    cw = cw_ref[...]
    conv = cb * (cw[0:1, :] * u2 + cw[1:2, :] * u1 + cw[2:3, :] * u)
    conv_ref[0] = conv.astype(BF16)
    carry_ref[...] = u[ts - SUBLANES:, :]


def _inproj(x, scale1, shift1, wqkv, wc, conv_w):
    b, s, d = x.shape
    ts = min(SEQ_TILE, s)
    vec = pl.BlockSpec((1, 1, d), lambda i, j: (i, 0, 0))
    full = lambda a: pl.BlockSpec(a.shape, lambda i, j: (0,) * a.ndim)
    tile = lambda w: pl.BlockSpec((1, ts, w), lambda i, j: (i, j, 0))
    act = lambda w: jax.ShapeDtypeStruct((b, s, w), BF16)
    return pl.pallas_call(
        _inproj_kernel,
        grid=(b, s // ts),
        in_specs=[tile(d), vec, vec, full(wqkv), full(wc), full(conv_w)],
        out_specs=[tile(D_ATTN), tile(D_ATTN), tile(D_ATTN), tile(D_CONV)],
        out_shape=[act(D_ATTN), act(D_ATTN), act(D_ATTN), act(D_CONV)],
        scratch_shapes=[pltpu.VMEM((SUBLANES, D_CONV), F32)],
        compiler_params=pltpu.CompilerParams(
            dimension_semantics=("arbitrary", "arbitrary"), vmem_limit_bytes=VMEM_LIMIT),
        name="inproj",
    )(x, scale1, shift1, wqkv, wc, conv_w)


def _attn_tile(qs, kj, vj, tri, carry, acc, causal):
    z = _dot_nt(qs, kj)
    sp = jnp.maximum(z, 0.0) + jnp.log(1.0 + jnp.exp2(-jnp.abs(z))) * LOG2E
    if causal is not None:
        sp = jnp.where(causal, sp, 0.0)
    sp16 = sp.astype(BF16)
    after = _dot(sp16, tri)
    a = jnp.exp2(z - (sp + (after + carry)))
    if causal is not None:
        a = jnp.where(causal, a, 0.0)
    acc = acc + _dot(a.astype(BF16), vj)
    carry = carry + (after[:, 0:1] + sp16[:, 0:1].astype(F32))
    return carry, acc


def _attn_kernel(q_ref, k_ref, v_ref, tri_ref, o_ref):
    s = q_ref.shape[1]
    bk = tri_ref.shape[0]
    bq = min(QUERY_BLOCK, s)
    per_q = bq // bk
    rows = 2 * bq
    pw = 2 * HEAD_DIM
    npairs = q_ref.shape[2] // pw
    lane = lax.broadcasted_iota(I32, (bq, pw), 1)

    def rows_from(x, r0):
        return x if r0 == 0 else jnp.concatenate([x[r0:bq], x[bq + r0:]], axis=0)

    def rows_into(full, part, r0):
        if r0 == 0:
            return part
        n = bq - r0
        return jnp.concatenate([full[:r0], part[:n], full[bq:bq + r0], part[n:]], axis=0)

    def causal_mask(r0):
        n = bq - r0
        assert n & (n - 1) == 0
        return (lax.broadcasted_iota(I32, (2 * n, bk), 1)
                < (lax.broadcasted_iota(I32, (2 * n, bk), 0) & (n - 1)))

    def q_block(qi, _):
        q0 = pl.multiple_of(qi * bq, bq)
        qs = []
        for p in range(npairs):
            q2 = q_ref[0, pl.ds(q0, bq), p * pw:(p + 1) * pw]
            zero = jnp.zeros_like(q2)
            qs.append(jnp.concatenate([jnp.where(lane < HEAD_DIM, q2, zero),
                                       jnp.where(lane >= HEAD_DIM, q2, zero)], axis=0))

        def tiles(j, state, causal=None, r0=0):
            k0 = pl.multiple_of(j * bk, bk)
            out = []
            for p in range(npairs):
                carry, acc = _attn_tile(rows_from(qs[p], r0),
                                        k_ref[0, pl.ds(k0, bk), p * pw:(p + 1) * pw],
                                        v_ref[0, pl.ds(k0, bk), p * pw:(p + 1) * pw], tri_ref[...],
                                        rows_from(state[2 * p], r0), rows_from(state[2 * p + 1], r0), causal)
                out.extend([rows_into(state[2 * p], carry, r0), rows_into(state[2 * p + 1], acc, r0)])
            return tuple(out)

        state = (jnp.zeros((rows, 1), F32), jnp.zeros((rows, pw), F32)) * npairs
        for d in reversed(range(per_q)):
            state = tiles(qi * per_q + d, state, causal_mask(d * bk), d * bk)
        def visible(t, st):
            for d in range(per_q):
                st = tiles((qi - t) * per_q - 1 - d, st)
            return st
        state = lax.fori_loop(0, qi, visible, state)
        for p in range(npairs):
            acc = state[2 * p + 1]
            o_ref[0, pl.ds(q0, bq), p * pw:(p + 1) * pw] = (
                jnp.where(lane < HEAD_DIM, acc[:bq], acc[bq:]).astype(BF16))
        return 0

    lax.fori_loop(0, s // bq, q_block, 0)


def _attn(q, k, v, tri):
    b, s, _ = q.shape
    width = PAIRS_PER_STEP * 2 * HEAD_DIM
    spec = pl.BlockSpec((1, s, width), lambda i, j: (i, 0, j))
    return pl.pallas_call(
        _attn_kernel,
        grid=(b, D_ATTN // width),
        in_specs=[spec, spec, spec, pl.BlockSpec(tri.shape, lambda i, j: (0, 0))],
        out_specs=spec,
        out_shape=jax.ShapeDtypeStruct((b, s, D_ATTN), BF16),
        compiler_params=pltpu.CompilerParams(
            dimension_semantics=("arbitrary", "arbitrary"), vmem_limit_bytes=VMEM_LIMIT),
        name="attn",
    )(q, k, v, tri)


def _outproj_kernel(attn_ref, conv_ref, x_ref, g1_ref, sc_ref, sh_ref, lg_ref, lb_ref,
                    wo_ref, wrh_ref, wrl_ref, rb_ref,
                    x1_ref, eid_ref, wcol_ref):
    ts = x_ref.shape[1]
    cat = jnp.concatenate([attn_ref[0], conv_ref[0]], axis=-1)
    mix = _dot(cat, wo_ref[...])
    x1 = _layer_norm(DEEPNORM_ALPHA * x_ref[0] + g1_ref[0] * mix, lg_ref[...], lb_ref[...])
    x1_ref[0] = x1
    h2 = x1 * (1.0 + sc_ref[0]) + sh_ref[0]

    h_hi, h_lo = _split_bf16(h2)
    wrh = wrh_ref[...]
    logit = _dot_nt(wrh, h_hi) + _dot_nt(wrl_ref[...], h_hi) + _dot_nt(wrh, h_lo) + rb_ref[...]

    e = EXPERTS_PER_GROUP
    rio = lax.broadcasted_iota(I32, (e, ts), 0)
    first_where = lambda hit: jnp.min(jnp.where(hit, rio, e), axis=0, keepdims=True)

    gl = logit[N_EXPERTS:N_EXPERTS + e]
    gmax = jnp.max(gl, axis=0, keepdims=True)
    g_prob = 1.0 / jnp.sum(jnp.exp(gl - gmax), axis=0, keepdims=True)
    g_idx = first_where(gl == gmax)
    sel = logit[0:e]
    for g in range(1, N_GROUPS):
        sel = jnp.where(g_idx == g, logit[g * e:(g + 1) * e], sel)
    m1 = jnp.max(sel, axis=0, keepdims=True)
    i1 = first_where(sel == m1)
    sel2 = jnp.where(rio == i1, -jnp.inf, sel)
    m2 = jnp.max(sel2, axis=0, keepdims=True)
    i2 = first_where(sel2 == m2)
    r = jnp.exp(m2 - m1)
    w1 = g_prob / (1.0 + r)
    w2 = w1 * r
    eid_ref[0:1, :] = g_idx * e + i1
    eid_ref[1:2, :] = g_idx * e + i2
    wrows = jnp.concatenate([w1, w2, jnp.zeros((LANES - 2, ts), F32)], axis=0)
    wcol_ref[...] = wrows.T


def _outproj(attn, conv, x, gate1, scale2, shift2, ln_g, ln_b, wo, wr_hi, wr_lo, rb):
    b, s, d = x.shape
    ts = min(SEQ_TILE, s)
    nj = s // ts
    vec = pl.BlockSpec((1, 1, d), lambda i, j: (i, 0, 0))
    full = lambda a: pl.BlockSpec(a.shape, lambda i, j: (0,) * a.ndim)
    tile = lambda w: pl.BlockSpec((1, ts, w), lambda i, j: (i, j, 0))
    return pl.pallas_call(
        _outproj_kernel,
        grid=(b, nj),
        in_specs=[tile(D_ATTN), tile(D_CONV), tile(d), vec, vec, vec, full(ln_g), full(ln_b),
                  full(wo), full(wr_hi), full(wr_lo), full(rb)],
        out_specs=[tile(d),
                   pl.BlockSpec((2, ts), lambda i, j: (0, i * nj + j)),
                   pl.BlockSpec((ts, LANES), lambda i, j: (i * nj + j, 0))],
        out_shape=[jax.ShapeDtypeStruct((b, s, d), F32),
                   jax.ShapeDtypeStruct((2, b * s), I32),
                   jax.ShapeDtypeStruct((b * s, LANES), F32)],
        compiler_params=pltpu.CompilerParams(
            dimension_semantics=("arbitrary", "arbitrary"), vmem_limit_bytes=VMEM_LIMIT),
        name="outproj",
    )(attn, conv, x, gate1, scale2, shift2, ln_g, ln_b, wo, wr_hi, wr_lo, rb)


def _pos_kernel(eid_ref, pos_ref, te_ref, meta_ref, *, row_tile):
    nchunk, ch = eid_ref.shape
    eio = lax.broadcasted_iota(I32, (N_EXPERTS, ch), 0)
    ones = jnp.ones((ch, LANES), BF16)
    tile_lanes = te_ref.shape[1]

    def onehot(c):
        return eid_ref[pl.ds(c, 1), :] == eio

    cnt = lax.fori_loop(0, nchunk, lambda c, acc: acc + _dot(onehot(c).astype(BF16), ones),
                        jnp.zeros((N_EXPERTS, LANES), F32))
    ntile = jnp.floor((cnt + (row_tile - 1)) * (1.0 / row_tile))
    lower = (lax.broadcasted_iota(I32, (N_EXPERTS, N_EXPERTS), 1)
             < lax.broadcasted_iota(I32, (N_EXPERTS, N_EXPERTS), 0)).astype(BF16)
    first_tile = _dot(lower, ntile.astype(BF16))
    end_tile = first_tile + ntile

    tio = lax.broadcasted_iota(I32, (N_EXPERTS, tile_lanes), 1).astype(F32)
    ends = jnp.concatenate([end_tile] * (tile_lanes // LANES), axis=1)
    owner = jnp.sum((ends <= tio).astype(F32), axis=0, keepdims=True)
    te_ref[...] = jnp.minimum(owner, N_EXPERTS - 1).astype(I32)
    meta_ref[0:N_EXPERTS, :] = (first_tile * row_tile + cnt).astype(I32)
    meta_ref[N_EXPERTS:2 * N_EXPERTS, :] = (ntile * row_tile - cnt).astype(I32)
    meta_ref[2 * N_EXPERTS:, :] = end_tile.astype(I32)

    before = (lax.broadcasted_iota(I32, (ch, ch), 0)
              < lax.broadcasted_iota(I32, (ch, ch), 1)).astype(BF16)

    def place(c, run):
        hit = onehot(c)
        hit16 = hit.astype(BF16)
        rank = _dot(hit16, before)
        base = jnp.concatenate([run] * (ch // LANES), axis=1)
        p = jnp.sum(jnp.where(hit, rank + base, 0.0), axis=0, keepdims=True)
        pos_ref[pl.ds(c, 1), :] = p.astype(I32)
        return run + _dot(hit16, ones)

    lax.fori_loop(0, nchunk, place, first_tile * row_tile)


def _positions(eid, n_tiles):
    npairs = eid.shape[0] * eid.shape[1]
    nchunk = npairs // POS_CHUNK
    tile_lanes = pl.cdiv(n_tiles, LANES) * LANES
    pos, te, meta = pl.pallas_call(
        functools.partial(_pos_kernel, row_tile=ROW_TILE),
        out_shape=[jax.ShapeDtypeStruct((nchunk, POS_CHUNK), I32),
                   jax.ShapeDtypeStruct((1, tile_lanes), I32),
                   jax.ShapeDtypeStruct((3 * N_EXPERTS, LANES), I32)],
        name="pos",
    )(eid.reshape(nchunk, POS_CHUNK))
    return pos.reshape(npairs), te.reshape(tile_lanes), meta


def _load_tokens(ref, base, n):
    return jnp.concatenate([ref[pl.ds(base + a, n, stride=SUBLANES), :] for a in range(SUBLANES)], axis=1)


def _store_tokens(ref, base, val):
    n = val.shape[0]
    for a in range(SUBLANES):
        ref[pl.ds(base + a, n, stride=SUBLANES), :] = val[:, a * LANES:(a + 1) * LANES]


def _dispatch_kernel(pos0_ref, pos1_ref, meta_ref, x1_ref, sc_ref, sh_ref, dst_hbm,
                     stage, zbuf, sem, zsem, *, row_tile, min_tiles):
    pos_refs = (pos0_ref, pos1_ref)
    tm = pos0_ref.shape[0]
    i = pl.program_id(0)
    n = pl.num_programs(0)
    cur = lax.rem(i, 2)
    n_tiles = dst_hbm.shape[0] // (row_tile * SUBLANES)

    def for_each_zero_copy(fn):
        def zero_copy(first_token, tokens):
            dst = pl.multiple_of(first_token * SUBLANES, SUBLANES)
            return pltpu.make_async_copy(zbuf.at[pl.ds(0, tokens * SUBLANES), :],
                                         dst_hbm.at[pl.ds(dst, tokens * SUBLANES), :], zsem)
        for e in range(N_EXPERTS):
            first = meta_ref[e, 0]
            npad = meta_ref[N_EXPERTS + e, 0]
            run = row_tile // 2
            while run >= 1:
                take = npad & run

                @pl.when(take != 0)
                def _(first=first, run=run):
                    fn(zero_copy(first, run))
                first = first + take
                run //= 2
        used = meta_ref[3 * N_EXPERTS - 1, 0]
        for t in range(min_tiles, n_tiles):
            @pl.when(t >= used)
            def _(t=t):
                fn(zero_copy(t * row_tile, row_tile))

    @pl.when(i == 0)
    def _():
        zbuf[...] = jnp.zeros_like(zbuf)
        for_each_zero_copy(lambda c: c.start())

    def token_copy(slot, r, buf):
        dst = pl.multiple_of(pos_refs[slot][r] * SUBLANES, SUBLANES)
        src = pl.multiple_of((buf * tm + r) * SUBLANES, SUBLANES)
        return pltpu.make_async_copy(stage.at[pl.ds(src, SUBLANES), :],
                                     dst_hbm.at[pl.ds(dst, SUBLANES), :], sem.at[buf])

    def drain(buf):
        def body(r, _):
            token_copy(0, r, buf).wait()
            token_copy(1, r, buf).wait()
            return 0
        lax.fori_loop(0, tm, body, 0, unroll=8)

    h2 = x1_ref[...] * (1.0 + sc_ref[0]) + sh_ref[0]
    _store_tokens(stage, cur * (tm * SUBLANES), h2)

    def issue(r, _):
        token_copy(0, r, cur).start(priority=0)
        token_copy(1, r, cur).start(priority=1)
        return 0
    lax.fori_loop(0, tm, issue, 0, unroll=8)

    @pl.when(i > 0)
    def _():
        drain(1 - cur)

    @pl.when(i == n - 1)
    def _():
        drain(cur)
        for_each_zero_copy(lambda c: c.wait())


def _dispatch(pos, meta, x1, scale2, shift2, n_tiles, seq):
    t, d = x1.shape
    assert d == SUBLANES * LANES
    tm = min(DISPATCH_TILE, seq)
    per_seq = seq // tm
    vec = pl.BlockSpec((1, 1, d), lambda i: (i // per_seq, 0, 0))
    nb = t // tm
    return pl.pallas_call(
        functools.partial(_dispatch_kernel, row_tile=ROW_TILE, min_tiles=(2 * t) // ROW_TILE),
        grid=(nb,),
        in_specs=[pl.BlockSpec((tm,), lambda i: (i,), memory_space=pltpu.SMEM),
                  pl.BlockSpec((tm,), lambda i: (nb + i,), memory_space=pltpu.SMEM),
                  pl.BlockSpec(memory_space=pltpu.SMEM),
                  pl.BlockSpec((tm, d), lambda i: (i, 0)), vec, vec],
        out_specs=pl.BlockSpec(memory_space=pl.ANY),
        out_shape=jax.ShapeDtypeStruct((n_tiles * ROW_TILE * SUBLANES, LANES), F32),
        scratch_shapes=[pltpu.VMEM((2 * tm * SUBLANES, LANES), F32),
                        pltpu.VMEM((ROW_TILE * SUBLANES, LANES), F32),
                        pltpu.SemaphoreType.DMA((2,)), pltpu.SemaphoreType.DMA],
        compiler_params=pltpu.CompilerParams(
            dimension_semantics=("arbitrary",), vmem_limit_bytes=VMEM_LIMIT),
        name="dispatch",
    )(pos, pos, meta, x1, scale2, shift2)


def _expert_kernel(te_ref, used_ref, xs_ref, wg_ref, wu_ref, wd_ref, ys_ref, wgu16, wd16):
    tm = xs_ref.shape[0] // SUBLANES
    t = pl.program_id(0)
    used = used_ref[0]
    last = used - 1
    expert = te_ref[jnp.minimum(t, last)]
    prev = te_ref[jnp.minimum(jnp.maximum(t - 1, 0), last)]

    @pl.when((t == 0) | (expert != prev))
    def _():
        wgu16[:, :D_EXPERT] = wg_ref[0].astype(BF16)
        wgu16[:, D_EXPERT:] = wu_ref[0].astype(BF16)
        wd16[...] = wd_ref[0].astype(BF16)

    @pl.when(t < used)
    def _():
        gu = _dot(_load_tokens(xs_ref, 0, tm).astype(BF16), wgu16[...])
        gate = gu[:, :D_EXPERT]
        hid = gate * (1.0 / (1.0 + jnp.exp(-gate))) * gu[:, D_EXPERT:]
        _store_tokens(ys_ref, 0, _dot(hid.astype(BF16), wd16[...]))

    @pl.when(t >= used)
    def _():
        ys_ref[...] = jnp.zeros_like(ys_ref)


def _experts(te, used, xs, w_gate, w_up, w_down):
    n_rows = xs.shape[0] // SUBLANES
    tm = ROW_TILE
    d = SUBLANES * LANES
    clamp = lambda t, used: jnp.minimum(t, used[0] - 1)
    weight = lambda shape: pl.BlockSpec((1,) + shape, lambda t, te, used: (te[clamp(t, used)], 0, 0))
    return pl.pallas_call(
        _expert_kernel,
        grid_spec=pltpu.PrefetchScalarGridSpec(
            num_scalar_prefetch=2,
            grid=(n_rows // tm,),
            in_specs=[pl.BlockSpec((tm * SUBLANES, LANES), lambda t, te, used: (clamp(t, used), 0)),
                      weight((d, D_EXPERT)), weight((d, D_EXPERT)), weight((D_EXPERT, d))],
            out_specs=pl.BlockSpec((tm * SUBLANES, LANES), lambda t, te, used: (t, 0)),
            scratch_shapes=[pltpu.VMEM((d, 2 * D_EXPERT), BF16), pltpu.VMEM((D_EXPERT, d), BF16)]),
        out_shape=jax.ShapeDtypeStruct(xs.shape, F32),
        compiler_params=pltpu.CompilerParams(
            dimension_semantics=("arbitrary",), vmem_limit_bytes=VMEM_LIMIT),
        name="expert",
    )(te, used, xs, w_gate, w_up, w_down)


def _combine_kernel(pos0_ref, pos1_ref, nxt0_ref, nxt1_ref, x1_ref, wcol_ref, g2_ref, lg_ref, lb_ref,
                    ys_hbm, o_ref, gbuf, sem):
    pos_ref = (pos0_ref, pos1_ref)
    posn_ref = (nxt0_ref, nxt1_ref)
    tm = pos0_ref.shape[0]
    i = pl.program_id(0)
    n = pl.num_programs(0)
    cur = lax.rem(i, 2)
    base = lambda buf, slot: (buf * 2 + slot) * (tm * SUBLANES)

    def token_copy(p_ref, slot, r, buf):
        src = pl.multiple_of(p_ref[slot][r] * SUBLANES, SUBLANES)
        dst = pl.multiple_of(base(buf, slot) + r * SUBLANES, SUBLANES)
        return pltpu.make_async_copy(ys_hbm.at[pl.ds(src, SUBLANES), :],
                                     gbuf.at[pl.ds(dst, SUBLANES), :], sem.at[buf])

    def issue(p_ref, buf):
        def body(r, _):
            token_copy(p_ref, 0, r, buf).start(priority=0)
            token_copy(p_ref, 1, r, buf).start(priority=1)
            return 0
        lax.fori_loop(0, tm, body, 0, unroll=8)

    @pl.when(i == 0)
    def _():
        issue(pos_ref, 0)

    @pl.when(i + 1 < n)
    def _():
        issue(posn_ref, 1 - cur)

    def drain(r, _):
        token_copy(pos_ref, 0, r, cur).wait()
        token_copy(pos_ref, 1, r, cur).wait()
        return 0
    lax.fori_loop(0, tm, drain, 0, unroll=8)

    wc = wcol_ref[...]
    moe = (wc[:, 0:1] * _load_tokens(gbuf, base(cur, 0), tm)
           + wc[:, 1:2] * _load_tokens(gbuf, base(cur, 1), tm))
    o_ref[...] = _layer_norm(DEEPNORM_ALPHA * x1_ref[...] + g2_ref[0] * moe, lg_ref[...], lb_ref[...])


def _combine(pos, x1, wcol, gate2, ln_g, ln_b, ys, seq):
    t, d = x1.shape
    tm = min(ROW_TILE, seq)
    n = t // tm
    per_seq = seq // tm
    full = lambda a: pl.BlockSpec(a.shape, lambda i: (0,) * a.ndim)
    smem = lambda index_map: pl.BlockSpec((tm,), index_map, memory_space=pltpu.SMEM)
    return pl.pallas_call(
        _combine_kernel,
        grid=(n,),
        in_specs=[smem(lambda i: (i,)), smem(lambda i: (n + i,)),
                  smem(lambda i: (jnp.minimum(i + 1, n - 1),)),
                  smem(lambda i: (n + jnp.minimum(i + 1, n - 1),)),
                  pl.BlockSpec((tm, d), lambda i: (i, 0)),
                  pl.BlockSpec((tm, LANES), lambda i: (i, 0)),
                  pl.BlockSpec((1, 1, d), lambda i: (i // per_seq, 0, 0)),
                  full(ln_g), full(ln_b),
                  pl.BlockSpec(memory_space=pl.ANY)],
        out_specs=pl.BlockSpec((tm, d), lambda i: (i, 0)),
        out_shape=jax.ShapeDtypeStruct((t, d), F32),
        scratch_shapes=[pltpu.VMEM((4 * tm * SUBLANES, LANES), F32), pltpu.SemaphoreType.DMA((2,))],
        compiler_params=pltpu.CompilerParams(
            dimension_semantics=("arbitrary",), vmem_limit_bytes=VMEM_LIMIT),
        name="combine",
    )(pos, pos, pos, pos, x1, wcol, gate2, ln_g, ln_b, ys)


def kernel(x, c, w_ada, b_ada, w_in, conv_w, w_out, ln1_g, ln1_b, w_router_group, b_router_group,
           w_router_expert, b_router_expert, w_gate, w_up, w_down, ln2_g, ln2_b):
    b, s, d = x.shape
    t = b * s
    assert d == D_ATTN + D_CONV and w_ada.shape[0] == DEPTH
    assert s % min(QUERY_BLOCK, s) == 0 and s % min(SEQ_TILE, s) == 0 and s % KEY_BLOCK == 0
    assert (2 * t) % POS_CHUNK == 0 and b <= SUBLANES

    c8 = jnp.zeros((SUBLANES, d), F32).at[:b].set(c)
    ada = _ada(c8, w_ada[0], b_ada[0][None, :])[:b].reshape(b, N_ADA, 1, d)
    shift1, scale1, gate1, shift2, scale2, gate2 = [ada[:, i] for i in range(N_ADA)]

    wi = w_in[0]
    q_scale = LOG2E * HEAD_DIM ** -0.5
    wqkv = jnp.concatenate([wi[:, :D_ATTN] * q_scale, wi[:, D_ATTN:3 * D_ATTN]], axis=1).astype(BF16)
    wc = wi[:, 3 * D_ATTN:].astype(BF16)
    wo = w_out[0].astype(BF16)
    wr = jnp.concatenate([w_router_expert[0].T, w_router_group[0].T,
                          jnp.zeros((ROUTER_ROWS - N_EXPERTS - N_GROUPS, d), F32)], axis=0)
    wr_hi = wr.astype(BF16)
    wr_lo = (wr - wr_hi.astype(F32)).astype(BF16)
    rb = jnp.concatenate([b_router_expert[0], b_router_group[0],
                          jnp.full((ROUTER_ROWS - N_EXPERTS - N_GROUPS,), -1e30, F32)])[:, None]
    tri = jnp.tril(jnp.ones((KEY_BLOCK, KEY_BLOCK), BF16), -1)

    q, k, v, conv = _inproj(x, scale1, shift1, wqkv, wc, conv_w[0])
    attn = _attn(q, k, v, tri)
    x1, eid, wcol = _outproj(attn, conv, x, gate1, scale2, shift2, ln1_g, ln1_b,
                             wo, wr_hi, wr_lo, rb)
    x1 = x1.reshape(t, d)

    n_tiles = (2 * t) // ROW_TILE + N_EXPERTS
    pos, te, meta = _positions(eid, n_tiles)
    xs = _dispatch(pos, meta, x1, scale2, shift2, n_tiles, s)
    ys = _experts(te, meta[3 * N_EXPERTS - 1, :1], xs, w_gate[0], w_up[0], w_down[0])
    out = _combine(pos, x1, wcol, gate2, ln2_g, ln2_b, ys, s)
    return out.reshape(b, s, d)
```

```python
import functools
import math

import jax
import jax.numpy as jnp
from jax import lax
from jax.experimental import pallas as pl
from jax.experimental.pallas import tpu as pltpu

F32 = jnp.float32
BF16 = jnp.bfloat16
I32 = jnp.int32

D_ATTN = 512
HEAD_DIM = 64
D_CONV = 512
CONV_WIDTH = 3
N_GROUPS = 4
EXPERTS_PER_GROUP = 8
N_EXPERTS = N_GROUPS * EXPERTS_PER_GROUP
D_EXPERT = 256
N_ADA = 6
DEPTH = 1
DEEPNORM_ALPHA = (2.0 * DEPTH) ** 0.25
LN_EPS = 1e-5
LOG2E = math.log2(math.e)

LANES = 128
SUBLANES = 8

SEQ_TILE = 512
KEY_BLOCK = 256
QUERY_BLOCK = 512
PAIRS_PER_STEP = 2
ROW_TILE = 256
SRC_CHUNK = 2048
POS_CHUNK = 256
ROUTER_ROWS = 40
VMEM_LIMIT = 48 * 1024 * 1024

_NT = (((1,), (1,)), ((), ()))


def _dot(a, b):
    return jnp.dot(a, b, preferred_element_type=F32)


def _dot_nt(a, b):
    return lax.dot_general(a, b, _NT, preferred_element_type=F32)


def _split_bf16(x):
    hi = x.astype(BF16)
    lo = (x - hi.astype(F32)).astype(BF16)
    return hi, lo


def _layer_norm(y, g, b):
    mu = jnp.mean(y, axis=-1, keepdims=True)
    yc = y - mu
    var = jnp.mean(yc * yc, axis=-1, keepdims=True)
    return yc * lax.rsqrt(var + LN_EPS) * g + b


def _ada_kernel(c_ref, w_ref, b_ref, o_ref):
    c = c_ref[...]
    s = c * (1.0 / (1.0 + jnp.exp(-c)))
    s_hi, s_lo = _split_bf16(s)
    w_hi, w_lo = _split_bf16(w_ref[...])
    o_ref[...] = _dot(s_hi, w_hi) + _dot(s_hi, w_lo) + _dot(s_lo, w_hi) + b_ref[...]


def _ada(c8, w, b):
    d = c8.shape[1]
    n = w.shape[1]
    return pl.pallas_call(
        _ada_kernel,
        grid=(n // d,),
        in_specs=[pl.BlockSpec((SUBLANES, d), lambda j: (0, 0)),
                  pl.BlockSpec((d, d), lambda j: (0, j)),
                  pl.BlockSpec((1, d), lambda j: (0, j))],
        out_specs=pl.BlockSpec((SUBLANES, d), lambda j: (0, j)),
        out_shape=jax.ShapeDtypeStruct((SUBLANES, n), F32),
        name="ada",
    )(c8, w, b)


def _inproj_kernel(x_ref, sc_ref, sh_ref, wqkv_ref, wc_ref, cw_ref,
                   q_ref, k_ref, v_ref, conv_ref, carry_ref):
    ts = x_ref.shape[1]

    @pl.when(pl.program_id(1) == 0)
    def _():
        carry_ref[...] = jnp.zeros_like(carry_ref)

    h = (x_ref[0] * (1.0 + sc_ref[0]) + sh_ref[0]).astype(BF16)
    qkv = _dot(h, wqkv_ref[...])
    q_ref[0] = qkv[:, :D_ATTN].astype(BF16)
    k_ref[0] = qkv[:, D_ATTN:2 * D_ATTN].astype(BF16)
    v_ref[0] = qkv[:, 2 * D_ATTN:].astype(BF16)

    c3 = _dot(h, wc_ref[...])
    cb = c3[:, :D_CONV]
    u = c3[:, D_CONV:2 * D_CONV] * c3[:, 2 * D_CONV:]
    prev = carry_ref[...]
    row = lax.broadcasted_iota(I32, u.shape, 0)
    u1 = jnp.where(row == 0, prev[SUBLANES - 1:SUBLANES, :], pltpu.roll(u, 1, 0))
    u2 = jnp.where(row == 0, prev[SUBLANES - 2:SUBLANES - 1, :],
                   jnp.where(row == 1, prev[SUBLANES - 1:SUBLANES, :], pltpu.roll(u, 2, 0)))
    cw = cw_ref[...]
    conv = cb * (cw[0:1, :] * u2 + cw[1:2, :] * u1 + cw[2:3, :] * u)
    conv_ref[0] = conv.astype(BF16)
    carry_ref[...] = u[ts - SUBLANES:, :]


def _inproj(x, scale1, shift1, wqkv, wc, conv_w):
    b, s, d = x.shape
    ts = min(SEQ_TILE, s)
    vec = pl.BlockSpec((1, 1, d), lambda i, j: (i, 0, 0))
    full = lambda a: pl.BlockSpec(a.shape, lambda i, j: (0,) * a.ndim)
    tile = lambda w: pl.BlockSpec((1, ts, w), lambda i, j: (i, j, 0))
    act = lambda w: jax.ShapeDtypeStruct((b, s, w), BF16)
    return pl.pallas_call(
        _inproj_kernel,
        grid=(b, s // ts),
        in_specs=[tile(d), vec, vec, full(wqkv), full(wc), full(conv_w)],
        out_specs=[tile(D_ATTN), tile(D_ATTN), tile(D_ATTN), tile(D_CONV)],
        out_shape=[act(D_ATTN), act(D_ATTN), act(D_ATTN), act(D_CONV)],
        scratch_shapes=[pltpu.VMEM((SUBLANES, D_CONV), F32)],
        compiler_params=pltpu.CompilerParams(
            dimension_semantics=("arbitrary", "arbitrary"), vmem_limit_bytes=VMEM_LIMIT),
        name="inproj",
    )(x, scale1, shift1, wqkv, wc, conv_w)


def _attn_tile(qs, kj, vj, tri, carry, acc, causal):
    z = _dot_nt(qs, kj)
    sp = jnp.maximum(z, 0.0) + jnp.log(1.0 + jnp.exp2(-jnp.abs(z))) * LOG2E
    if causal is not None:
        sp = jnp.where(causal, sp, 0.0)
    sp16 = sp.astype(BF16)
    after = _dot(sp16, tri)
    a = jnp.exp2(z - (sp + (after + carry)))
    if causal is not None:
        a = jnp.where(causal, a, 0.0)
    acc = acc + _dot(a.astype(BF16), vj)
    carry = carry + (after[:, 0:1] + sp16[:, 0:1].astype(F32))
    return carry, acc


def _attn_kernel(q_ref, k_ref, v_ref, tri_ref, o_ref):
    s = q_ref.shape[1]
    bk = tri_ref.shape[0]
    bq = min(QUERY_BLOCK, s)
    per_q = bq // bk
    rows = 2 * bq
    pw = 2 * HEAD_DIM
    npairs = q_ref.shape[2] // pw
    lane = lax.broadcasted_iota(I32, (bq, pw), 1)

    def rows_from(x, r0):
        return x if r0 == 0 else jnp.concatenate([x[r0:bq], x[bq + r0:]], axis=0)

    def rows_into(full, part, r0):
        if r0 == 0:
            return part
        n = bq - r0
        return jnp.concatenate([full[:r0], part[:n], full[bq:bq + r0], part[n:]], axis=0)

    def causal_mask(r0):
        n = bq - r0
        assert n & (n - 1) == 0
        return (lax.broadcasted_iota(I32, (2 * n, bk), 1)
                < (lax.broadcasted_iota(I32, (2 * n, bk), 0) & (n - 1)))

    def q_block(qi, _):
        q0 = pl.multiple_of(qi * bq, bq)
        qs = []
        for p in range(npairs):
            q2 = q_ref[0, pl.ds(q0, bq), p * pw:(p + 1) * pw]
            zero = jnp.zeros_like(q2)
            qs.append(jnp.concatenate([jnp.where(lane < HEAD_DIM, q2, zero),
                                       jnp.where(lane >= HEAD_DIM, q2, zero)], axis=0))

        def tiles(j, state, causal=None, r0=0):
            k0 = pl.multiple_of(j * bk, bk)
            out = []
            for p in range(npairs):
                carry, acc = _attn_tile(rows_from(qs[p], r0),
                                        k_ref[0, pl.ds(k0, bk), p * pw:(p + 1) * pw],
                                        v_ref[0, pl.ds(k0, bk), p * pw:(p + 1) * pw], tri_ref[...],
                                        rows_from(state[2 * p], r0), rows_from(state[2 * p + 1], r0), causal)
                out.extend([rows_into(state[2 * p], carry, r0), rows_into(state[2 * p + 1], acc, r0)])
            return tuple(out)

        state = (jnp.zeros((rows, 1), F32), jnp.zeros((rows, pw), F32)) * npairs
        for d in reversed(range(per_q)):
            state = tiles(qi * per_q + d, state, causal_mask(d * bk), d * bk)
        def visible(t, st):
            for d in range(per_q):
                st = tiles((qi - t) * per_q - 1 - d, st)
            return st
        state = lax.fori_loop(0, qi, visible, state)
        for p in range(npairs):
            acc = state[2 * p + 1]
            o_ref[0, pl.ds(q0, bq), p * pw:(p + 1) * pw] = (
                jnp.where(lane < HEAD_DIM, acc[:bq], acc[bq:]).astype(BF16))
        return 0

    lax.fori_loop(0, s // bq, q_block, 0)


def _attn(q, k, v, tri):
    b, s, _ = q.shape
    width = PAIRS_PER_STEP * 2 * HEAD_DIM
    spec = pl.BlockSpec((1, s, width), lambda i, j: (i, 0, j))
    return pl.pallas_call(
        _attn_kernel,
        grid=(b, D_ATTN // width),
        in_specs=[spec, spec, spec, pl.BlockSpec(tri.shape, lambda i, j: (0, 0))],
        out_specs=spec,
        out_shape=jax.ShapeDtypeStruct((b, s, D_ATTN), BF16),
        compiler_params=pltpu.CompilerParams(
            dimension_semantics=("arbitrary", "arbitrary"), vmem_limit_bytes=VMEM_LIMIT),
        name="attn",
    )(q, k, v, tri)


def _outproj_kernel(attn_ref, conv_ref, x_ref, g1_ref, sc_ref, sh_ref, lg_ref, lb_ref,
                    wo_ref, wrh_ref, wrl_ref, rb_ref,
                    x1_ref, h2t_ref, eid_ref, wcol_ref):
    ts = x_ref.shape[1]
    cat = jnp.concatenate([attn_ref[0], conv_ref[0]], axis=-1)
    mix = _dot(cat, wo_ref[...])
    x1 = _layer_norm(DEEPNORM_ALPHA * x_ref[0] + g1_ref[0] * mix, lg_ref[...], lb_ref[...])
    x1_ref[0] = x1
    h2 = x1 * (1.0 + sc_ref[0]) + sh_ref[0]
    _store_tokens(h2t_ref, 0, h2)

    h_hi, h_lo = _split_bf16(h2)
    wrh = wrh_ref[...]
    logit = _dot_nt(wrh, h_hi) + _dot_nt(wrl_ref[...], h_hi) + _dot_nt(wrh, h_lo) + rb_ref[...]

    e = EXPERTS_PER_GROUP
    rio = lax.broadcasted_iota(I32, (e, ts), 0)
    first_where = lambda hit: jnp.min(jnp.where(hit, rio, e), axis=0, keepdims=True)

    gl = logit[N_EXPERTS:N_EXPERTS + e]
    gmax = jnp.max(gl, axis=0, keepdims=True)
    g_prob = 1.0 / jnp.sum(jnp.exp(gl - gmax), axis=0, keepdims=True)
    g_idx = first_where(gl == gmax)
    sel = logit[0:e]
    for g in range(1, N_GROUPS):
        sel = jnp.where(g_idx == g, logit[g * e:(g + 1) * e], sel)
    m1 = jnp.max(sel, axis=0, keepdims=True)
    i1 = first_where(sel == m1)
    sel2 = jnp.where(rio == i1, -jnp.inf, sel)
    m2 = jnp.max(sel2, axis=0, keepdims=True)
    i2 = first_where(sel2 == m2)
    r = jnp.exp(m2 - m1)
    w1 = g_prob / (1.0 + r)
    w2 = w1 * r
    eid_ref[0:1, :] = g_idx * e + i1
    eid_ref[1:2, :] = g_idx * e + i2
    wrows = jnp.concatenate([w1, w2, jnp.zeros((LANES - 2, ts), F32)], axis=0)
    wcol_ref[...] = wrows.T


def _outproj(attn, conv, x, gate1, scale2, shift2, ln_g, ln_b, wo, wr_hi, wr_lo, rb):
    b, s, d = x.shape
    ts = min(SEQ_TILE, s)
    nj = s // ts
    vec = pl.BlockSpec((1, 1, d), lambda i, j: (i, 0, 0))
    full = lambda a: pl.BlockSpec(a.shape, lambda i, j: (0,) * a.ndim)
    tile = lambda w: pl.BlockSpec((1, ts, w), lambda i, j: (i, j, 0))
    return pl.pallas_call(
        _outproj_kernel,
        grid=(b, nj),
        in_specs=[tile(D_ATTN), tile(D_CONV), tile(d), vec, vec, vec, full(ln_g), full(ln_b),
                  full(wo), full(wr_hi), full(wr_lo), full(rb)],
        out_specs=[tile(d),
                   pl.BlockSpec((ts * SUBLANES, LANES), lambda i, j: (i * nj + j, 0)),
                   pl.BlockSpec((2, ts), lambda i, j: (0, i * nj + j)),
                   pl.BlockSpec((ts, LANES), lambda i, j: (i * nj + j, 0))],
        out_shape=[jax.ShapeDtypeStruct((b, s, d), F32),
                   jax.ShapeDtypeStruct((b * s * SUBLANES, LANES), F32),
                   jax.ShapeDtypeStruct((2, b * s), I32),
                   jax.ShapeDtypeStruct((b * s, LANES), F32)],
        compiler_params=pltpu.CompilerParams(
            dimension_semantics=("arbitrary", "arbitrary"), vmem_limit_bytes=VMEM_LIMIT),
        name="outproj",
    )(attn, conv, x, gate1, scale2, shift2, ln_g, ln_b, wo, wr_hi, wr_lo, rb)


def _pos_kernel(eid_ref, pos_ref, te_ref, meta_ref, *, row_tile):
    nchunk, ch = eid_ref.shape
    eio = lax.broadcasted_iota(I32, (N_EXPERTS, ch), 0)
    ones = jnp.ones((ch, LANES), BF16)
    tile_lanes = te_ref.shape[1]

    def onehot(c):
        return eid_ref[pl.ds(c, 1), :] == eio

    cnt = lax.fori_loop(0, nchunk, lambda c, acc: acc + _dot(onehot(c).astype(BF16), ones),
                        jnp.zeros((N_EXPERTS, LANES), F32))
    ntile = jnp.floor((cnt + (row_tile - 1)) * (1.0 / row_tile))
    lower = (lax.broadcasted_iota(I32, (N_EXPERTS, N_EXPERTS), 1)
             < lax.broadcasted_iota(I32, (N_EXPERTS, N_EXPERTS), 0)).astype(BF16)
    first_tile = _dot(lower, ntile.astype(BF16))
    end_tile = first_tile + ntile

    tio = lax.broadcasted_iota(I32, (N_EXPERTS, tile_lanes), 1).astype(F32)
    ends = jnp.concatenate([end_tile] * (tile_lanes // LANES), axis=1)
    owner = jnp.sum((ends <= tio).astype(F32), axis=0, keepdims=True)
    te_ref[...] = jnp.minimum(owner, N_EXPERTS - 1).astype(I32)
    meta_ref[0:N_EXPERTS, :] = (first_tile * row_tile + cnt).astype(I32)
    meta_ref[N_EXPERTS:2 * N_EXPERTS, :] = (ntile * row_tile - cnt).astype(I32)
    meta_ref[2 * N_EXPERTS:, :] = end_tile.astype(I32)

    before = (lax.broadcasted_iota(I32, (ch, ch), 0)
              < lax.broadcasted_iota(I32, (ch, ch), 1)).astype(BF16)

    def place(c, run):
        hit = onehot(c)
        hit16 = hit.astype(BF16)
        rank = _dot(hit16, before)
        base = jnp.concatenate([run] * (ch // LANES), axis=1)
        p = jnp.sum(jnp.where(hit, rank + base, 0.0), axis=0, keepdims=True)
        pos_ref[pl.ds(c, 1), :] = p.astype(I32)
        return run + _dot(hit16, ones)

    lax.fori_loop(0, nchunk, place, first_tile * row_tile)


def _positions(eid, n_tiles):
    npairs = eid.shape[0] * eid.shape[1]
    nchunk = npairs // POS_CHUNK
    tile_lanes = pl.cdiv(n_tiles, LANES) * LANES
    pos, te, meta = pl.pallas_call(
        functools.partial(_pos_kernel, row_tile=ROW_TILE),
        out_shape=[jax.ShapeDtypeStruct((nchunk, POS_CHUNK), I32),
                   jax.ShapeDtypeStruct((1, tile_lanes), I32),
                   jax.ShapeDtypeStruct((3 * N_EXPERTS, LANES), I32)],
        name="pos",
    )(eid.reshape(nchunk, POS_CHUNK))
    return pos.reshape(npairs), te.reshape(tile_lanes), meta


def _load_tokens(ref, base, n):
    return jnp.concatenate([ref[pl.ds(base + a, n, stride=SUBLANES), :] for a in range(SUBLANES)], axis=1)


def _store_tokens(ref, base, val):
    n = val.shape[0]
    for a in range(SUBLANES):
        ref[pl.ds(base + a, n, stride=SUBLANES), :] = val[:, a * LANES:(a + 1) * LANES]


def _src_kernel(pos_ref, meta_ref, src_ref, *, n_tokens, row_tile):
    i = pl.program_id(0)
    chunk = pos_ref.shape[0]
    n_rows = src_ref.shape[0]

    @pl.when(i == 0)
    def _():
        def fill(first, end):
            first = first & ~(SUBLANES - 1)

            def body(k, _):
                for j in range(SUBLANES):
                    src_ref[first + k * SUBLANES + j] = 0
                return 0
            lax.fori_loop(0, (end - first) // SUBLANES, body, 0)
        for e in range(N_EXPERTS):
            fill(meta_ref[e], meta_ref[e] + meta_ref[N_EXPERTS + e])
        fill(meta_ref[3 * N_EXPERTS - 1] * row_tile, n_rows)

    tok0 = lax.rem(i * chunk, n_tokens)

    def body(r, _):
        src_ref[pos_ref[r]] = tok0 + r
        return 0
    lax.fori_loop(0, chunk, body, 0, unroll=8)


def _sources(pos, meta, n_rows, n_tokens):
    chunk = min(SRC_CHUNK, n_tokens)
    assert n_tokens % chunk == 0
    return pl.pallas_call(
        functools.partial(_src_kernel, n_tokens=n_tokens, row_tile=ROW_TILE),
        grid=(pos.shape[0] // chunk,),
        in_specs=[pl.BlockSpec((chunk,), lambda i: (i,), memory_space=pltpu.SMEM),
                  pl.BlockSpec(memory_space=pltpu.SMEM)],
        out_specs=pl.BlockSpec(memory_space=pltpu.SMEM),
        out_shape=jax.ShapeDtypeStruct((n_rows,), I32),
        compiler_params=pltpu.CompilerParams(dimension_semantics=("arbitrary",)),
        name="src",
    )(pos, meta[:, 0])


def _expert_kernel(te_ref, src_ref, nxt_ref, h2_hbm, wg_ref, wu_ref, wd_ref, ys_ref,
                   gbuf0, gbuf1, wgu16, wd16, sem):
    tm = src_ref.shape[0]
    t = pl.program_id(0)
    n = pl.num_programs(0)
    bufs = (gbuf0, gbuf1)

    def token_copy(s_ref, r, b):
        src = pl.multiple_of(s_ref[r] * SUBLANES, SUBLANES)
        return pltpu.make_async_copy(h2_hbm.at[pl.ds(src, SUBLANES), :],
                                     bufs[b].at[pl.ds(r * SUBLANES, SUBLANES), :], sem.at[b])

    def drain(b):
        def body(r, _):
            token_copy(src_ref, r, b).wait()
            return 0
        lax.fori_loop(0, tm, body, 0, unroll=8)

    @pl.when(t == 0)
    def _():
        def body(r, _):
            token_copy(src_ref, r, 0).start()
            return 0
        lax.fori_loop(0, tm, body, 0, unroll=8)

    @pl.when((t == 0) | (te_ref[t] != te_ref[jnp.maximum(t - 1, 0)]))
    def _():
        wgu16[:, :D_EXPERT] = wg_ref[0].astype(BF16)
        wgu16[:, D_EXPERT:] = wu_ref[0].astype(BF16)
        wd16[...] = wd_ref[0].astype(BF16)

    def step(b):
        drain(b)
        for r in range(tm):
            token_copy(nxt_ref, r, 1 - b).start(priority=r % 2)
        x = _load_tokens(bufs[b], 0, tm).astype(BF16)
        gu = _dot(x, wgu16[...])
        gate = gu[:, :D_EXPERT]
        hid = gate * (1.0 / (1.0 + jnp.exp(-gate))) * gu[:, D_EXPERT:]
        _store_tokens(ys_ref, 0, _dot(hid.astype(BF16), wd16[...]))

        @pl.when(t == n - 1)
        def _():
            drain(1 - b)

    for b in range(2):
        pl.when(lax.rem(t, 2) == b)(functools.partial(step, b))


def _experts(te, src, h2t, w_gate, w_up, w_down):
    tm = ROW_TILE
    n = src.shape[0] // tm
    d = SUBLANES * LANES
    weight = lambda shape: pl.BlockSpec((1,) + shape, lambda t, te: (te[t], 0, 0))
    rows = pltpu.VMEM((tm * SUBLANES, LANES), F32)
    return pl.pallas_call(
        _expert_kernel,
        grid_spec=pltpu.PrefetchScalarGridSpec(
            num_scalar_prefetch=1,
            grid=(n,),
            in_specs=[pl.BlockSpec((tm,), lambda t, te: (t,), memory_space=pltpu.SMEM),
                      pl.BlockSpec((tm,), lambda t, te: (jnp.minimum(t + 1, n - 1),),
                                   memory_space=pltpu.SMEM),
                      pl.BlockSpec(memory_space=pl.ANY),
                      weight((d, D_EXPERT)), weight((d, D_EXPERT)), weight((D_EXPERT, d))],
            out_specs=pl.BlockSpec((tm * SUBLANES, LANES), lambda t, te: (t, 0)),
            scratch_shapes=[rows, rows,
                            pltpu.VMEM((d, 2 * D_EXPERT), BF16), pltpu.VMEM((D_EXPERT, d), BF16),
                            pltpu.SemaphoreType.DMA((2,))]),
        out_shape=jax.ShapeDtypeStruct((n * tm * SUBLANES, LANES), F32),
        compiler_params=pltpu.CompilerParams(
            dimension_semantics=("arbitrary",), vmem_limit_bytes=VMEM_LIMIT),
        name="expert",
    )(te, src, src, h2t, w_gate, w_up, w_down)


def _combine_kernel(pos0_ref, pos1_ref, nxt0_ref, nxt1_ref, x1_ref, wcol_ref, g2_ref, lg_ref, lb_ref,
                    ys_hbm, o_ref, gbuf, sem):
    pos_ref = (pos0_ref, pos1_ref)
    posn_ref = (nxt0_ref, nxt1_ref)
    tm = pos0_ref.shape[0]
    i = pl.program_id(0)
    n = pl.num_programs(0)
    cur = lax.rem(i, 2)
    base = lambda buf, slot: (buf * 2 + slot) * (tm * SUBLANES)

    def token_copy(p_ref, slot, r, buf):
        src = pl.multiple_of(p_ref[slot][r] * SUBLANES, SUBLANES)
        dst = pl.multiple_of(base(buf, slot) + r * SUBLANES, SUBLANES)
        return pltpu.make_async_copy(ys_hbm.at[pl.ds(src, SUBLANES), :],
                                     gbuf.at[pl.ds(dst, SUBLANES), :], sem.at[buf])

    def issue(p_ref, buf):
        def body(r, _):
            token_copy(p_ref, 0, r, buf).start(priority=0)
            token_copy(p_ref, 1, r, buf).start(priority=1)
            return 0
        lax.fori_loop(0, tm, body, 0, unroll=8)

    @pl.when(i == 0)
    def _():
        issue(pos_ref, 0)

    @pl.when(i + 1 < n)
    def _():
        issue(posn_ref, 1 - cur)

    def drain(r, _):
        token_copy(pos_ref, 0, r, cur).wait()
        token_copy(pos_ref, 1, r, cur).wait()
        return 0
    lax.fori_loop(0, tm, drain, 0, unroll=8)

    wc = wcol_ref[...]
    moe = (wc[:, 0:1] * _load_tokens(gbuf, base(cur, 0), tm)
           + wc[:, 1:2] * _load_tokens(gbuf, base(cur, 1), tm))
    o_ref[...] = _layer_norm(DEEPNORM_ALPHA * x1_ref[...] + g2_ref[0] * moe, lg_ref[...], lb_ref[...])


def _combine(pos, x1, wcol, gate2, ln_g, ln_b, ys, seq):
    t, d = x1.shape
    tm = min(ROW_TILE, seq)
    n = t // tm
    per_seq = seq // tm
    full = lambda a: pl.BlockSpec(a.shape, lambda i: (0,) * a.ndim)
    smem = lambda index_map: pl.BlockSpec((tm,), index_map, memory_space=pltpu.SMEM)
    return pl.pallas_call(
        _combine_kernel,
        grid=(n,),
        in_specs=[smem(lambda i: (i,)), smem(lambda i: (n + i,)),
                  smem(lambda i: (jnp.minimum(i + 1, n - 1),)),
                  smem(lambda i: (n + jnp.minimum(i + 1, n - 1),)),
                  pl.BlockSpec((tm, d), lambda i: (i, 0)),
                  pl.BlockSpec((tm, LANES), lambda i: (i, 0)),
                  pl.BlockSpec((1, 1, d), lambda i: (i // per_seq, 0, 0)),
                  full(ln_g), full(ln_b),
                  pl.BlockSpec(memory_space=pl.ANY)],
        out_specs=pl.BlockSpec((tm, d), lambda i: (i, 0)),
        out_shape=jax.ShapeDtypeStruct((t, d), F32),
        scratch_shapes=[pltpu.VMEM((4 * tm * SUBLANES, LANES), F32), pltpu.SemaphoreType.DMA((2,))],
        compiler_params=pltpu.CompilerParams(
            dimension_semantics=("arbitrary",), vmem_limit_bytes=VMEM_LIMIT),
        name="combine",
    )(pos, pos, pos, pos, x1, wcol, gate2, ln_g, ln_b, ys)


def kernel(x, c, w_ada, b_ada, w_in, conv_w, w_out, ln1_g, ln1_b, w_router_group, b_router_group,
           w_router_expert, b_router_expert, w_gate, w_up, w_down, ln2_g, ln2_b):
    b, s, d = x.shape
    t = b * s
    assert d == D_ATTN + D_CONV and w_ada.shape[0] == DEPTH
    assert s % min(QUERY_BLOCK, s) == 0 and s % min(SEQ_TILE, s) == 0 and s % KEY_BLOCK == 0
    assert (2 * t) % POS_CHUNK == 0 and b <= SUBLANES

    c8 = jnp.zeros((SUBLANES, d), F32).at[:b].set(c)
    ada = _ada(c8, w_ada[0], b_ada[0][None, :])[:b].reshape(b, N_ADA, 1, d)
    shift1, scale1, gate1, shift2, scale2, gate2 = [ada[:, i] for i in range(N_ADA)]

    wi = w_in[0]
    q_scale = LOG2E * HEAD_DIM ** -0.5
    wqkv = jnp.concatenate([wi[:, :D_ATTN] * q_scale, wi[:, D_ATTN:3 * D_ATTN]], axis=1).astype(BF16)
    wc = wi[:, 3 * D_ATTN:].astype(BF16)
    wo = w_out[0].astype(BF16)
    wr = jnp.concatenate([w_router_expert[0].T, w_router_group[0].T,
                          jnp.zeros((ROUTER_ROWS - N_EXPERTS - N_GROUPS, d), F32)], axis=0)
    wr_hi = wr.astype(BF16)
    wr_lo = (wr - wr_hi.astype(F32)).astype(BF16)
    rb = jnp.concatenate([b_router_expert[0], b_router_group[0],
                          jnp.full((ROUTER_ROWS - N_EXPERTS - N_GROUPS,), -1e30, F32)])[:, None]
    tri = jnp.tril(jnp.ones((KEY_BLOCK, KEY_BLOCK), BF16), -1)

    q, k, v, conv = _inproj(x, scale1, shift1, wqkv, wc, conv_w[0])
    attn = _attn(q, k, v, tri)
    x1, h2t, eid, wcol = _outproj(attn, conv, x, gate1, scale2, shift2, ln1_g, ln1_b,
                             wo, wr_hi, wr_lo, rb)
    x1 = x1.reshape(t, d)

    n_tiles = (2 * t) // ROW_TILE + N_EXPERTS
    pos, te, meta = _positions(eid, n_tiles)
    src = _sources(pos, meta, n_tiles * ROW_TILE, t)
    ys = _experts(te, src, h2t, w_gate[0], w_up[0], w_down[0])
    out = _combine(pos, x1, wcol, gate2, ln2_g, ln2_b, ys, s)
    return out.reshape(b, s, d)
```

```python
import functools
import math

import jax
import jax.numpy as jnp
from jax import lax
from jax.experimental import pallas as pl
from jax.experimental.pallas import tpu as pltpu

F32 = jnp.float32
BF16 = jnp.bfloat16
I32 = jnp.int32

D_ATTN = 512
HEAD_DIM = 64
D_CONV = 512
CONV_WIDTH = 3
N_GROUPS = 4
EXPERTS_PER_GROUP = 8
N_EXPERTS = N_GROUPS * EXPERTS_PER_GROUP
D_EXPERT = 256
N_ADA = 6
DEPTH = 1
DEEPNORM_ALPHA = (2.0 * DEPTH) ** 0.25
LN_EPS = 1e-5
LOG2E = math.log2(math.e)

LANES = 128
SUBLANES = 8

SEQ_TILE = 1024
KEY_BLOCK = 256
QUERY_BLOCK = 1024
KEY_BLOCKS_PER_ITER = 4
PAIRS_PER_STEP = 2
ROW_TILE = 512
COMBINE_TILE = 256
DISPATCH_TILE = 512
XS_SLOTS = 3
POS_CHUNK = 1024
ROUTER_ROWS = 40
VMEM_LIMIT = 48 * 1024 * 1024
ATTN_VMEM_LIMIT = 60 * 1024 * 1024

_NT = (((1,), (1,)), ((), ()))


def _dot(a, b):
    return jnp.dot(a, b, preferred_element_type=F32)


def _dot_nt(a, b):
    return lax.dot_general(a, b, _NT, preferred_element_type=F32)


def _split_bf16(x):
    hi = x.astype(BF16)
    lo = (x - hi.astype(F32)).astype(BF16)
    return hi, lo


def _layer_norm(y, g, b):
    mu = jnp.mean(y, axis=-1, keepdims=True)
    yc = y - mu
    var = jnp.mean(yc * yc, axis=-1, keepdims=True)
    return yc * lax.rsqrt(var + LN_EPS) * g + b


def _ada_kernel(c_ref, w_ref, b_ref, o_ref):
    c = c_ref[...]
    s = c * (1.0 / (1.0 + jnp.exp(-c)))
    s_hi, s_lo = _split_bf16(s)
    w_hi, w_lo = _split_bf16(w_ref[...])
    o_ref[...] = _dot(s_hi, w_hi) + _dot(s_hi, w_lo) + _dot(s_lo, w_hi) + b_ref[...]


def _ada(c8, w, b):
    d = c8.shape[1]
    n = w.shape[1]
    return pl.pallas_call(
        _ada_kernel,
        grid=(n // d,),
        in_specs=[pl.BlockSpec((SUBLANES, d), lambda j: (0, 0)),
                  pl.BlockSpec((d, d), lambda j: (0, j)),
                  pl.BlockSpec((1, d), lambda j: (0, j))],
        out_specs=pl.BlockSpec((SUBLANES, d), lambda j: (0, j)),
        out_shape=jax.ShapeDtypeStruct((SUBLANES, n), F32),
        name="ada",
    )(c8, w, b)


def _inproj_kernel(x_ref, sc_ref, sh_ref, wqkv_ref, wc_ref, cw_ref,
                   q_ref, k_ref, v_ref, conv_ref, carry_ref):
    ts = x_ref.shape[1]

    @pl.when(pl.program_id(1) == 0)
    def _():
        carry_ref[...] = jnp.zeros_like(carry_ref)

    h = (x_ref[0] * (1.0 + sc_ref[0]) + sh_ref[0]).astype(BF16)
    qkv = _dot(h, wqkv_ref[...])
    q_ref[0] = qkv[:, :D_ATTN].astype(BF16)
    k_ref[0] = qkv[:, D_ATTN:2 * D_ATTN].astype(BF16)
    v_ref[0] = qkv[:, 2 * D_ATTN:].astype(BF16)

    c3 = _dot(h, wc_ref[...])
    cb = c3[:, :D_CONV]
    u = c3[:, D_CONV:2 * D_CONV] * c3[:, 2 * D_CONV:]
    prev = carry_ref[...]
    row = lax.broadcasted_iota(I32, u.shape, 0)
    u1 = jnp.where(row == 0, prev[SUBLANES - 1:SUBLANES, :], pltpu.roll(u, 1, 0))
    u2 = jnp.where(row == 0, prev[SUBLANES - 2:SUBLANES - 1, :],
                   jnp.where(row == 1, prev[SUBLANES - 1:SUBLANES, :], pltpu.roll(u, 2, 0)))
    cw = cw_ref[...]
    conv = cb * (cw[0:1, :] * u2 + cw[1:2, :] * u1 + cw[2:3, :] * u)
    conv_ref[0] = conv.astype(BF16)
    carry_ref[...] = u[ts - SUBLANES:, :]


def _inproj(x, scale1, shift1, wqkv, wc, conv_w):
    b, s, d = x.shape
    ts = min(SEQ_TILE, s)
    vec = pl.BlockSpec((1, 1, d), lambda i, j: (i, 0, 0))
    full = lambda a: pl.BlockSpec(a.shape, lambda i, j: (0,) * a.ndim, pipeline_mode=pl.Buffered(1))
    tile = lambda w: pl.BlockSpec((1, ts, w), lambda i, j: (i, j, 0))
    act = lambda w: jax.ShapeDtypeStruct((b, s, w), BF16)
    return pl.pallas_call(
        _inproj_kernel,
        grid=(b, s // ts),
        in_specs=[tile(d), vec, vec, full(wqkv), full(wc), full(conv_w)],
        out_specs=[tile(D_ATTN), tile(D_ATTN), tile(D_ATTN), tile(D_CONV)],
        out_shape=[act(D_ATTN), act(D_ATTN), act(D_ATTN), act(D_CONV)],
        scratch_shapes=[pltpu.VMEM((SUBLANES, D_CONV), F32)],
        compiler_params=pltpu.CompilerParams(
            dimension_semantics=("arbitrary", "arbitrary"), vmem_limit_bytes=VMEM_LIMIT),
        name="inproj",
    )(x, scale1, shift1, wqkv, wc, conv_w)


def _attn_tile(qs, kj, vj, tri, carry, acc, causal):
    z = _dot_nt(qs, kj)
    sp = jnp.maximum(z, 0.0) + jnp.log(1.0 + jnp.exp2(-jnp.abs(z))) * LOG2E
    if causal is not None:
        sp = jnp.where(causal, sp, 0.0)
    sp16 = sp.astype(BF16)
    after = _dot(sp16, tri)
    a = jnp.exp2(z - (sp + (after + carry)))
    if causal is not None:
        a = jnp.where(causal, a, 0.0)
    acc = acc + _dot(a.astype(BF16), vj)
    carry = carry + (after[:, 0:1] + sp16[:, 0:1].astype(F32))
    return carry, acc


def _attn_kernel(q_ref, k_ref, v_ref, tri_ref, o_ref):
    s = q_ref.shape[1]
    bk = tri_ref.shape[0]
    bq = min(QUERY_BLOCK, s)
    per_q = bq // bk
    rows = 2 * bq
    pw = 2 * HEAD_DIM
    npairs = q_ref.shape[2] // pw
    lane = lax.broadcasted_iota(I32, (bq, pw), 1)

    def rows_from(x, r0):
        return x if r0 == 0 else jnp.concatenate([x[r0:bq], x[bq + r0:]], axis=0)

    def rows_into(full, part, r0):
        if r0 == 0:
            return part
        n = bq - r0
        return jnp.concatenate([full[:r0], part[:n], full[bq:bq + r0], part[n:]], axis=0)

    def causal_mask(r0):
        n = bq - r0
        row = lax.broadcasted_iota(I32, (2 * n, bk), 0)
        return lax.broadcasted_iota(I32, (2 * n, bk), 1) < jnp.where(row >= n, row - n, row)

    def q_block(qi, _):
        q0 = pl.multiple_of(qi * bq, bq)
        qs = []
        for p in range(npairs):
            q2 = q_ref[0, pl.ds(q0, bq), p * pw:(p + 1) * pw]
            zero = jnp.zeros_like(q2)
            qs.append(jnp.concatenate([jnp.where(lane < HEAD_DIM, q2, zero),
                                       jnp.where(lane >= HEAD_DIM, q2, zero)], axis=0))

        def tiles(j, state, causal=None, r0=0):
            k0 = pl.multiple_of(j * bk, bk)
            out = []
            for p in range(npairs):
                carry, acc = _attn_tile(rows_from(qs[p], r0),
                                        k_ref[0, pl.ds(k0, bk), p * pw:(p + 1) * pw],
                                        v_ref[0, pl.ds(k0, bk), p * pw:(p + 1) * pw], tri_ref[...],
                                        rows_from(state[2 * p], r0), rows_from(state[2 * p + 1], r0), causal)
                out.extend([rows_into(state[2 * p], carry, r0), rows_into(state[2 * p + 1], acc, r0)])
            return tuple(out)

        state = (jnp.zeros((rows, 1), F32), jnp.zeros((rows, pw), F32)) * npairs
        for d in reversed(range(per_q)):
            state = tiles(qi * per_q + d, state, causal_mask(d * bk), d * bk)
        group = math.gcd(per_q, KEY_BLOCKS_PER_ITER)

        def visible(t, st):
            for d in range(group):
                st = tiles(qi * per_q - 1 - t * group - d, st)
            return st
        state = lax.fori_loop(0, qi * (per_q // group), visible, state)
        for p in range(npairs):
            acc = state[2 * p + 1]
            o_ref[0, pl.ds(q0, bq), p * pw:(p + 1) * pw] = (
                jnp.where(lane < HEAD_DIM, acc[:bq], acc[bq:]).astype(BF16))
        return 0

    lax.fori_loop(0, s // bq, q_block, 0)


def _attn(q, k, v, tri):
    b, s, _ = q.shape
    width = PAIRS_PER_STEP * 2 * HEAD_DIM
    spec = pl.BlockSpec((1, s, width), lambda i, j: (i, 0, j))
    return pl.pallas_call(
        _attn_kernel,
        grid=(b, D_ATTN // width),
        in_specs=[spec, spec, spec, pl.BlockSpec(tri.shape, lambda i, j: (0, 0))],
        out_specs=spec,
        out_shape=jax.ShapeDtypeStruct((b, s, D_ATTN), BF16),
        compiler_params=pltpu.CompilerParams(
            dimension_semantics=("arbitrary", "arbitrary"), vmem_limit_bytes=ATTN_VMEM_LIMIT),
        name="attn",
    )(q, k, v, tri)


def _outproj_kernel(attn_ref, conv_ref, x_ref, g1_ref, sc_ref, sh_ref, lg_ref, lb_ref,
                    wo_ref, wrh_ref, wrl_ref, rb_ref,
                    x1_ref, eid_ref, wcol_ref):
    ts = x_ref.shape[1]
    cat = jnp.concatenate([attn_ref[0], conv_ref[0]], axis=-1)
    mix = _dot(cat, wo_ref[...])
    x1 = _layer_norm(DEEPNORM_ALPHA * x_ref[0] + g1_ref[0] * mix, lg_ref[...], lb_ref[...])
    x1_ref[0] = x1
    h2 = x1 * (1.0 + sc_ref[0]) + sh_ref[0]

    h_hi, h_lo = _split_bf16(h2)
    wrh = wrh_ref[...]
    logit = _dot_nt(wrh, h_hi) + _dot_nt(wrl_ref[...], h_hi) + _dot_nt(wrh, h_lo) + rb_ref[...]

    e = EXPERTS_PER_GROUP
    rio = lax.broadcasted_iota(I32, (e, ts), 0)
    first_where = lambda hit: jnp.min(jnp.where(hit, rio, e), axis=0, keepdims=True)

    gl = logit[N_EXPERTS:N_EXPERTS + e]
    gmax = jnp.max(gl, axis=0, keepdims=True)
    g_prob = 1.0 / jnp.sum(jnp.exp(gl - gmax), axis=0, keepdims=True)
    g_idx = first_where(gl == gmax)
    sel = logit[0:e]
    for g in range(1, N_GROUPS):
        sel = jnp.where(g_idx == g, logit[g * e:(g + 1) * e], sel)
    m1 = jnp.max(sel, axis=0, keepdims=True)
    i1 = first_where(sel == m1)
    sel2 = jnp.where(rio == i1, -jnp.inf, sel)
    m2 = jnp.max(sel2, axis=0, keepdims=True)
    i2 = first_where(sel2 == m2)
    r = jnp.exp(m2 - m1)
    w1 = g_prob / (1.0 + r)
    w2 = w1 * r
    eid_ref[0:1, :] = g_idx * e + i1
    eid_ref[1:2, :] = g_idx * e + i2
    wrows = jnp.concatenate([w1, w2, jnp.zeros((LANES - 2, ts), F32)], axis=0)
    wcol_ref[...] = wrows.T


def _outproj(attn, conv, x, gate1, scale2, shift2, ln_g, ln_b, wo, wr_hi, wr_lo, rb):
    b, s, d = x.shape
    ts = min(SEQ_TILE, s)
    nj = s // ts
    vec = pl.BlockSpec((1, 1, d), lambda i, j: (i, 0, 0))
    full = lambda a: pl.BlockSpec(a.shape, lambda i, j: (0,) * a.ndim, pipeline_mode=pl.Buffered(1))
    tile = lambda w: pl.BlockSpec((1, ts, w), lambda i, j: (i, j, 0))
    return pl.pallas_call(
        _outproj_kernel,
        grid=(b, nj),
        in_specs=[tile(D_ATTN), tile(D_CONV), tile(d), vec, vec, vec, full(ln_g), full(ln_b),
                  full(wo), full(wr_hi), full(wr_lo), full(rb)],
        out_specs=[tile(d),
                   pl.BlockSpec((2, ts), lambda i, j: (0, i * nj + j)),
                   pl.BlockSpec((ts, LANES), lambda i, j: (i * nj + j, 0))],
        out_shape=[jax.ShapeDtypeStruct((b, s, d), F32),
                   jax.ShapeDtypeStruct((2, b * s), I32),
                   jax.ShapeDtypeStruct((b * s, LANES), F32)],
        compiler_params=pltpu.CompilerParams(
            dimension_semantics=("arbitrary", "arbitrary"), vmem_limit_bytes=VMEM_LIMIT),
        name="outproj",
    )(attn, conv, x, gate1, scale2, shift2, ln_g, ln_b, wo, wr_hi, wr_lo, rb)


def _pos_kernel(eid_ref, pos_ref, te_ref, meta_ref, *, row_tile):
    nchunk, ch = eid_ref.shape
    eio = lax.broadcasted_iota(I32, (N_EXPERTS, ch), 0)
    ones = jnp.ones((ch, LANES), BF16)
    tile_lanes = te_ref.shape[1]

    def onehot(c):
        return eid_ref[pl.ds(c, 1), :] == eio

    cnt = lax.fori_loop(0, nchunk, lambda c, acc: acc + _dot(onehot(c).astype(BF16), ones),
                        jnp.zeros((N_EXPERTS, LANES), F32))
    ntile = jnp.floor((cnt + (row_tile - 1)) * (1.0 / row_tile))
    lower = (lax.broadcasted_iota(I32, (N_EXPERTS, N_EXPERTS), 1)
             < lax.broadcasted_iota(I32, (N_EXPERTS, N_EXPERTS), 0)).astype(BF16)
    first_tile = _dot(lower, ntile.astype(BF16))
    end_tile = first_tile + ntile

    tio = lax.broadcasted_iota(I32, (N_EXPERTS, tile_lanes), 1).astype(F32)
    ends = jnp.concatenate([end_tile] * (tile_lanes // LANES), axis=1)
    owner = jnp.sum((ends <= tio).astype(F32), axis=0, keepdims=True)
    te_ref[...] = jnp.minimum(owner, N_EXPERTS - 1).astype(I32)
    meta_ref[0:N_EXPERTS, :] = (first_tile * row_tile + cnt).astype(I32)
    meta_ref[N_EXPERTS:2 * N_EXPERTS, :] = (ntile * row_tile - cnt).astype(I32)
    meta_ref[2 * N_EXPERTS:, :] = end_tile.astype(I32)

    before = (lax.broadcasted_iota(I32, (ch, ch), 0)
              < lax.broadcasted_iota(I32, (ch, ch), 1)).astype(BF16)

    def place(c, run):
        hit = onehot(c)
        hit16 = hit.astype(BF16)
        rank = _dot(hit16, before)
        base = jnp.concatenate([run] * (ch // LANES), axis=1)
        p = jnp.sum(jnp.where(hit, rank + base, 0.0), axis=0, keepdims=True)
        pos_ref[pl.ds(c, 1), :] = p.astype(I32)
        return run + _dot(hit16, ones)

    lax.fori_loop(0, nchunk, place, first_tile * row_tile)


def _positions(eid, n_tiles):
    npairs = eid.shape[0] * eid.shape[1]
    nchunk = npairs // POS_CHUNK
    tile_lanes = pl.cdiv(n_tiles, LANES) * LANES
    pos, te, meta = pl.pallas_call(
        functools.partial(_pos_kernel, row_tile=ROW_TILE),
        out_shape=[jax.ShapeDtypeStruct((nchunk, POS_CHUNK), I32),
                   jax.ShapeDtypeStruct((1, tile_lanes), I32),
                   jax.ShapeDtypeStruct((3 * N_EXPERTS, LANES), I32)],
        name="pos",
    )(eid.reshape(nchunk, POS_CHUNK))
    return pos.reshape(npairs), te.reshape(tile_lanes), meta


def _load_tokens(ref, base, n):
    return jnp.concatenate([ref[pl.ds(base + a, n, stride=SUBLANES), :] for a in range(SUBLANES)], axis=1)


def _store_tokens(ref, base, val):
    n = val.shape[0]
    for a in range(SUBLANES):
        ref[pl.ds(base + a, n, stride=SUBLANES), :] = val[:, a * LANES:(a + 1) * LANES]


def _dispatch_kernel(pos0_ref, pos1_ref, meta_ref, x1_ref, sc_ref, sh_ref, dst_hbm,
                     stage, zbuf, sem, zsem, *, row_tile, min_tiles):
    pos_refs = (pos0_ref, pos1_ref)
    tm = pos0_ref.shape[0]
    i = pl.program_id(0)
    n = pl.num_programs(0)
    cur = lax.rem(i, 2)
    n_tiles = dst_hbm.shape[0] // (row_tile * SUBLANES)

    def for_each_zero_copy(fn):
        def zero_copy(first_token, tokens):
            dst = pl.multiple_of(first_token * SUBLANES, SUBLANES)
            return pltpu.make_async_copy(zbuf.at[pl.ds(0, tokens * SUBLANES), :],
                                         dst_hbm.at[pl.ds(dst, tokens * SUBLANES), :], zsem)
        for e in range(N_EXPERTS):
            first = meta_ref[e, 0]
            npad = meta_ref[N_EXPERTS + e, 0]
            run = row_tile // 2
            while run >= 1:
                take = npad & run

                @pl.when(take != 0)
                def _(first=first, run=run):
                    fn(zero_copy(first, run))
                first = first + take
                run //= 2
        used = meta_ref[3 * N_EXPERTS - 1, 0]
        for t in range(min_tiles, n_tiles):
            @pl.when(t >= used)
            def _(t=t):
                fn(zero_copy(t * row_tile, row_tile))

    @pl.when(i == 0)
    def _():
        zbuf[...] = jnp.zeros_like(zbuf)
        for_each_zero_copy(lambda c: c.start())

    def token_copy(slot, r, buf):
        dst = pl.multiple_of(pos_refs[slot][r] * SUBLANES, SUBLANES)
        src = pl.multiple_of((buf * tm + r) * SUBLANES, SUBLANES)
        return pltpu.make_async_copy(stage.at[pl.ds(src, SUBLANES), :],
                                     dst_hbm.at[pl.ds(dst, SUBLANES), :], sem.at[buf])

    def drain(buf):
        def body(r, _):
            token_copy(0, r, buf).wait()
            token_copy(1, r, buf).wait()
            return 0
        lax.fori_loop(0, tm, body, 0, unroll=8)

    h2 = x1_ref[...] * (1.0 + sc_ref[0]) + sh_ref[0]
    _store_tokens(stage, cur * (tm * SUBLANES), h2)

    def issue(r, _):
        token_copy(0, r, cur).start(priority=0)
        token_copy(1, r, cur).start(priority=1)
        return 0
    lax.fori_loop(0, tm, issue, 0, unroll=8)

    @pl.when(i > 0)
    def _():
        drain(1 - cur)

    @pl.when(i == n - 1)
    def _():
        drain(cur)
        for_each_zero_copy(lambda c: c.wait())


def _dispatch(pos, meta, x1, scale2, shift2, n_tiles, seq):
    t, d = x1.shape
    assert d == SUBLANES * LANES
    tm = min(DISPATCH_TILE, seq)
    per_seq = seq // tm
    vec = pl.BlockSpec((1, 1, d), lambda i: (i // per_seq, 0, 0))
    nb = t // tm
    return pl.pallas_call(
        functools.partial(_dispatch_kernel, row_tile=ROW_TILE, min_tiles=(2 * t) // ROW_TILE),
        grid=(nb,),
        in_specs=[pl.BlockSpec((tm,), lambda i: (i,), memory_space=pltpu.SMEM),
                  pl.BlockSpec((tm,), lambda i: (nb + i,), memory_space=pltpu.SMEM),
                  pl.BlockSpec(memory_space=pltpu.SMEM),
                  pl.BlockSpec((tm, d), lambda i: (i, 0)), vec, vec],
        out_specs=pl.BlockSpec(memory_space=pl.ANY),
        out_shape=jax.ShapeDtypeStruct((n_tiles * ROW_TILE * SUBLANES, LANES), F32),
        scratch_shapes=[pltpu.VMEM((2 * tm * SUBLANES, LANES), F32),
                        pltpu.VMEM((ROW_TILE * SUBLANES, LANES), F32),
                        pltpu.SemaphoreType.DMA((2,)), pltpu.SemaphoreType.DMA],
        compiler_params=pltpu.CompilerParams(
            dimension_semantics=("arbitrary",), vmem_limit_bytes=VMEM_LIMIT),
        name="dispatch",
    )(pos, pos, meta, x1, scale2, shift2)


def _expert_kernel(te_ref, used_ref, xs_hbm, wg_ref, wu_ref, wd_ref, ys_ref, xbuf, wgu16, wd16, sem):
    tile_rows = ys_ref.shape[0]
    tm = tile_rows // SUBLANES
    t = pl.program_id(0)
    used = used_ref[0]
    last = used - 1
    expert = te_ref[jnp.minimum(t, last)]
    prev = te_ref[jnp.minimum(jnp.maximum(t - 1, 0), last)]

    def tile_copy(tile, slot):
        src = pl.multiple_of(tile * tile_rows, tile_rows)
        dst = pl.multiple_of(slot * tile_rows, tile_rows)
        return pltpu.make_async_copy(xs_hbm.at[pl.ds(src, tile_rows), :],
                                     xbuf.at[pl.ds(dst, tile_rows), :], sem.at[slot])

    @pl.when(t == 0)
    def _():
        for k in range(XS_SLOTS - 1):
            @pl.when(k < used)
            def _(k=k):
                tile_copy(k, k).start()

    ahead = t + (XS_SLOTS - 1)

    @pl.when(ahead < used)
    def _():
        tile_copy(ahead, lax.rem(ahead, XS_SLOTS)).start()

    @pl.when((t == 0) | (expert != prev))
    def _():
        wgu16[:, :D_EXPERT] = wg_ref[0].astype(BF16)
        wgu16[:, D_EXPERT:] = wu_ref[0].astype(BF16)
        wd16[...] = wd_ref[0].astype(BF16)

    @pl.when(t < used)
    def _():
        slot = lax.rem(t, XS_SLOTS)
        tile_copy(t, slot).wait()
        x = _load_tokens(xbuf, slot * tile_rows, tm).astype(BF16)
        gu = _dot(x, wgu16[...])
        gate = gu[:, :D_EXPERT]
        hid = gate * (1.0 / (1.0 + jnp.exp(-gate))) * gu[:, D_EXPERT:]
        _store_tokens(ys_ref, 0, _dot(hid.astype(BF16), wd16[...]))

    @pl.when(t >= used)
    def _():
        ys_ref[...] = jnp.zeros_like(ys_ref)


def _experts(te, used, xs, w_gate, w_up, w_down):
    n_rows = xs.shape[0] // SUBLANES
    tm = ROW_TILE
    d = SUBLANES * LANES
    clamp = lambda t, used: jnp.minimum(t, used[0] - 1)
    weight = lambda shape: pl.BlockSpec((1,) + shape, lambda t, te, used: (te[clamp(t, used)], 0, 0))
    return pl.pallas_call(
        _expert_kernel,
        grid_spec=pltpu.PrefetchScalarGridSpec(
            num_scalar_prefetch=2,
            grid=(n_rows // tm,),
            in_specs=[pl.BlockSpec(memory_space=pl.ANY),
                      weight((d, D_EXPERT)), weight((d, D_EXPERT)), weight((D_EXPERT, d))],
            out_specs=pl.BlockSpec((tm * SUBLANES, LANES), lambda t, te, used: (t, 0)),
            scratch_shapes=[pltpu.VMEM((XS_SLOTS * tm * SUBLANES, LANES), F32),
                            pltpu.VMEM((d, 2 * D_EXPERT), BF16), pltpu.VMEM((D_EXPERT, d), BF16),
                            pltpu.SemaphoreType.DMA((XS_SLOTS,))]),
        out_shape=jax.ShapeDtypeStruct(xs.shape, F32),
        compiler_params=pltpu.CompilerParams(
            dimension_semantics=("arbitrary",), vmem_limit_bytes=VMEM_LIMIT),
        name="expert",
    )(te, used, xs, w_gate, w_up, w_down)


def _combine_kernel(pos0_ref, pos1_ref, nxt0_ref, nxt1_ref, x1_ref, wcol_ref, g2_ref, lg_ref, lb_ref,
                    ys_hbm, o_ref, gbuf, sem):
    pos_ref = (pos0_ref, pos1_ref)
    posn_ref = (nxt0_ref, nxt1_ref)
    tm = pos0_ref.shape[0]
    i = pl.program_id(0)
    n = pl.num_programs(0)
    cur = lax.rem(i, 2)
    base = lambda buf, slot: (buf * 2 + slot) * (tm * SUBLANES)

    def token_copy(p_ref, slot, r, buf):
        src = pl.multiple_of(p_ref[slot][r] * SUBLANES, SUBLANES)
        dst = pl.multiple_of(base(buf, slot) + r * SUBLANES, SUBLANES)
        return pltpu.make_async_copy(ys_hbm.at[pl.ds(src, SUBLANES), :],
                                     gbuf.at[pl.ds(dst, SUBLANES), :], sem.at[buf])

    def issue(p_ref, buf):
        def body(r, _):
            token_copy(p_ref, 0, r, buf).start(priority=0)
            token_copy(p_ref, 1, r, buf).start(priority=1)
            return 0
        lax.fori_loop(0, tm, body, 0, unroll=8)

    @pl.when(i == 0)
    def _():
        issue(pos_ref, 0)

    @pl.when(i + 1 < n)
    def _():
        issue(posn_ref, 1 - cur)

    def drain(r, _):
        token_copy(pos_ref, 0, r, cur).wait()
        token_copy(pos_ref, 1, r, cur).wait()
        return 0
    lax.fori_loop(0, tm, drain, 0, unroll=8)

    wc = wcol_ref[...]
    moe = (wc[:, 0:1] * _load_tokens(gbuf, base(cur, 0), tm)
           + wc[:, 1:2] * _load_tokens(gbuf, base(cur, 1), tm))
    o_ref[...] = _layer_norm(DEEPNORM_ALPHA * x1_ref[...] + g2_ref[0] * moe, lg_ref[...], lb_ref[...])


def _combine(pos, x1, wcol, gate2, ln_g, ln_b, ys, seq):
    t, d = x1.shape
    tm = min(COMBINE_TILE, seq)
    n = t // tm
    per_seq = seq // tm
    full = lambda a: pl.BlockSpec(a.shape, lambda i: (0,) * a.ndim)
    smem = lambda index_map: pl.BlockSpec((tm,), index_map, memory_space=pltpu.SMEM)
    return pl.pallas_call(
        _combine_kernel,
        grid=(n,),
        in_specs=[smem(lambda i: (i,)), smem(lambda i: (n + i,)),
                  smem(lambda i: (jnp.minimum(i + 1, n - 1),)),
                  smem(lambda i: (n + jnp.minimum(i + 1, n - 1),)),
                  pl.BlockSpec((tm, d), lambda i: (i, 0)),
                  pl.BlockSpec((tm, LANES), lambda i: (i, 0)),
                  pl.BlockSpec((1, 1, d), lambda i: (i // per_seq, 0, 0)),
                  full(ln_g), full(ln_b),
                  pl.BlockSpec(memory_space=pl.ANY)],
        out_specs=pl.BlockSpec((tm, d), lambda i: (i, 0)),
        out_shape=jax.ShapeDtypeStruct((t, d), F32),
        scratch_shapes=[pltpu.VMEM((4 * tm * SUBLANES, LANES), F32), pltpu.SemaphoreType.DMA((2,))],
        compiler_params=pltpu.CompilerParams(
            dimension_semantics=("arbitrary",), vmem_limit_bytes=VMEM_LIMIT),
        name="combine",
    )(pos, pos, pos, pos, x1, wcol, gate2, ln_g, ln_b, ys)


def kernel(x, c, w_ada, b_ada, w_in, conv_w, w_out, ln1_g, ln1_b, w_router_group, b_router_group,
           w_router_expert, b_router_expert, w_gate, w_up, w_down, ln2_g, ln2_b):
    b, s, d = x.shape
    t = b * s
    assert d == D_ATTN + D_CONV and w_ada.shape[0] == DEPTH
    assert s % min(QUERY_BLOCK, s) == 0 and s % min(SEQ_TILE, s) == 0 and s % KEY_BLOCK == 0
    assert (2 * t) % POS_CHUNK == 0 and b <= SUBLANES

    c8 = jnp.zeros((SUBLANES, d), F32).at[:b].set(c)
    ada = _ada(c8, w_ada[0], b_ada[0][None, :])[:b].reshape(b, N_ADA, 1, d)
    shift1, scale1, gate1, shift2, scale2, gate2 = [ada[:, i] for i in range(N_ADA)]

    wi = w_in[0]
    q_scale = LOG2E * HEAD_DIM ** -0.5
    wqkv = jnp.concatenate([wi[:, :D_ATTN] * q_scale, wi[:, D_ATTN:3 * D_ATTN]], axis=1).astype(BF16)
    wc = wi[:, 3 * D_ATTN:].astype(BF16)
    wo = w_out[0].astype(BF16)
    wr = jnp.concatenate([w_router_expert[0].T, w_router_group[0].T,
                          jnp.zeros((ROUTER_ROWS - N_EXPERTS - N_GROUPS, d), F32)], axis=0)
    wr_hi = wr.astype(BF16)
    wr_lo = (wr - wr_hi.astype(F32)).astype(BF16)
    rb = jnp.concatenate([b_router_expert[0], b_router_group[0],
                          jnp.full((ROUTER_ROWS - N_EXPERTS - N_GROUPS,), -1e30, F32)])[:, None]
    tri = jnp.tril(jnp.ones((KEY_BLOCK, KEY_BLOCK), BF16), -1)

    q, k, v, conv = _inproj(x, scale1, shift1, wqkv, wc, conv_w[0])
    attn = _attn(q, k, v, tri)
    x1, eid, wcol = _outproj(attn, conv, x, gate1, scale2, shift2, ln1_g, ln1_b,
                             wo, wr_hi, wr_lo, rb)
    x1 = x1.reshape(t, d)

    n_tiles = (2 * t) // ROW_TILE + N_EXPERTS
    pos, te, meta = _positions(eid, n_tiles)
    xs = _dispatch(pos, meta, x1, scale2, shift2, n_tiles, s)
    ys = _experts(te, meta[3 * N_EXPERTS - 1, :1], xs, w_gate[0], w_up[0], w_down[0])
    out = _combine(pos, x1, wcol, gate2, ln2_g, ln2_b, ys, s)
    return out.reshape(b, s, d)
```
